```python
import jax, jax.numpy as jnp
from jax import lax
import numpy as np

D_MODEL = 4096
BATCH = 1
SEQ = 16384
DEPTH = 1
DEC_BATCH = 32
DEC_SEQ = 32
PAST_LEN = 4096

CHUNK = 64
N_META = 16
SB_HEADS = 16
SB_HEAD_DIM = 128
ML_HEADS = 4
ML_QK_DIM = 256
ML_V_DIM = 512
D_FF = 11008
CONV_W = 3
QBLOCK = 128
KBLOCK = 128
GATE_CAP = 15.0
NORM_EPS = 1e-6

SB_W = SB_HEADS * SB_HEAD_DIM
ML_QK_W = ML_HEADS * ML_QK_DIM
ML_W = ML_HEADS * ML_V_DIM
PROJ_SPLITS = (SB_W, SB_W, SB_W, ML_QK_W, ML_QK_W, ML_W, ML_W, ML_HEADS, ML_HEADS)
IN_COLS = 3 * SB_W + 2 * ML_QK_W + 2 * ML_W + 2 * ML_HEADS

kernel_name = 'hymba_stickbreak_mlstm_convffn_stream_step'


def rmsnorm(x, g):
    xf = x.astype(jnp.float32)
    y = xf * lax.rsqrt(jnp.mean(xf * xf, axis=-1, keepdims=True) + NORM_EPS)
    return (y * g.astype(jnp.float32)).astype(x.dtype)


def soft_cap(x):
    return GATE_CAP * jnp.tanh(x / GATE_CAP)


def rev_excl_cumsum(x):
    n = x.shape[-1]
    nkb = n // KBLOCK
    xb = x.reshape(*x.shape[:-1], nkb, KBLOCK)
    r = jnp.arange(KBLOCK)
    m_in = (r[:, None] > r[None, :]).astype(x.dtype)
    within = jnp.einsum('...nj,js->...ns', xb, m_in, precision=lax.Precision.HIGHEST)
    tot = jnp.sum(xb, axis=-1)
    c = jnp.arange(nkb)
    m_blk = (c[:, None] > c[None, :]).astype(x.dtype)
    after = jnp.einsum('...c,cb->...b', tot, m_blk, precision=lax.Precision.HIGHEST)
    return (within + after[..., None]).reshape(x.shape)


def stick_breaking_attention(q, k, v, k_offset):
    B, Lq, H, Dh = q.shape
    Lk = k.shape[1]
    qb = min(QBLOCK, Lq)
    kf = k.astype(jnp.float32)
    vf = v.astype(jnp.float32)
    scale = Dh ** -0.5
    outs = []
    for start in range(0, Lq, qb):
        end = min(start + qb, Lq)
        kl = min(k_offset + end, Lk)
        klp = -(-kl // KBLOCK) * KBLOCK
        kb = jnp.pad(kf[:, :kl], ((0, 0), (0, klp - kl), (0, 0), (0, 0)))
        vb = jnp.pad(vf[:, :kl], ((0, 0), (0, klp - kl), (0, 0), (0, 0)))
        qi = q[:, start:end].astype(jnp.float32)
        z = jnp.einsum('bqhd,bkhd->bhqk', qi, kb) * scale
        qpos = k_offset + jnp.arange(start, end)
        valid = (jnp.arange(klp)[None, :] < qpos[:, None])[None, None]
        ls_neg = jnp.where(valid, jax.nn.log_sigmoid(-z), 0.0)
        after = rev_excl_cumsum(ls_neg)
        a = jnp.where(valid, jnp.exp(z + ls_neg + after), 0.0)
        outs.append(jnp.einsum('bhqk,bkhd->bqhd', a, vb))
    return jnp.concatenate(outs, axis=1).astype(q.dtype)


def mlstm_block(state, blk):
    C, n, m = state
    q, k, v, logi, logf = blk
    L = q.shape[1]
    b = jnp.cumsum(logf, axis=1)
    causal = jnp.tril(jnp.ones((L, L), dtype=bool))[None, :, :, None]
    d = jnp.where(causal, b[:, :, None, :] - b[:, None, :, :] + logi[:, None, :, :], -jnp.inf)
    g = b + m[:, None, :]
    m_t = jnp.maximum(g, jnp.max(d, axis=2))
    w_intra = jnp.exp(d - m_t[:, :, None, :])
    w_inter = jnp.exp(g - m_t)
    qs = q * (ML_QK_DIM ** -0.5)
    qk = jnp.einsum('bthd,bshd->btsh', qs, k) * w_intra
    num = jnp.einsum('btsh,bshe->bthe', qk, v) + w_inter[..., None] * jnp.einsum('bthd,bhde->bthe', qs, C)
    den = jnp.sum(qk, axis=2) + w_inter * jnp.einsum('bthd,bhd->bth', qs, n)
    h = num / jnp.maximum(jnp.abs(den), jnp.exp(-m_t))[..., None]
    m_new = m_t[:, -1]
    w_end = jnp.exp(b[:, -1:, :] - b + logi - m_new[:, None, :])
    decay = jnp.exp(g[:, -1] - m_new)
    C_new = decay[..., None, None] * C + jnp.einsum('bsh,bshd,bshe->bhde', w_end, k, v)
    n_new = decay[..., None] * n + jnp.einsum('bsh,bshd->bhd', w_end, k)
    return (C_new, n_new, m_new), h


def mlstm_sequence(state, q, k, v, logi, logf, lead):
    outs = []
    if lead > 0:
        state, h0 = mlstm_block(state, (q[:, :lead], k[:, :lead], v[:, :lead], logi[:, :lead], logf[:, :lead]))
        outs.append(h0)
    rest = [a[:, lead:] for a in (q, k, v, logi, logf)]
    Bsz, R = rest[0].shape[0], rest[0].shape[1]
    blk = min(CHUNK, R)
    nblk = R // blk
    blocks = tuple(jnp.moveaxis(a.reshape(Bsz, nblk, blk, *a.shape[2:]), 1, 0) for a in rest)
    state, hb = lax.scan(mlstm_block, state, blocks)
    outs.append(jnp.moveaxis(hb, 0, 1).reshape(Bsz, R, ML_HEADS, ML_V_DIM))
    return state, jnp.concatenate(outs, axis=1)


def conv_ffn(xn, conv_state, w_up, conv_w, w_down):
    u = xn @ w_up
    L = u.shape[1]
    ext = jnp.concatenate([conv_state.astype(u.dtype), u], axis=1)
    c = ext[:, 0:L] * conv_w[0]
    for j in range(1, CONV_W):
        c = c + ext[:, j:j + L] * conv_w[j]
    gate, val = jnp.split(c, 2, axis=-1)
    y = (jax.nn.silu(gate) * val) @ w_down
    return y, ext[:, -(CONV_W - 1):]


def trunk_layer(h, k_offset, past_k, past_v, C0, n0, m0, conv0, lead,
                n1g, w_in, b_ig, b_fg, sb_g, ml_g, w_out, n2g, w_up, cw, w_down):
    Bsz, L, _ = h.shape
    xn = rmsnorm(h, n1g)
    proj = xn @ w_in
    idx = [int(i) for i in np.cumsum(PROJ_SPLITS)[:-1]]
    p_sq, p_sk, p_sv, p_mq, p_mk, p_mv, p_mo, p_mi, p_mf = jnp.split(proj, idx, axis=-1)
    sq = p_sq.reshape(Bsz, L, SB_HEADS, SB_HEAD_DIM)
    sk = p_sk.reshape(Bsz, L, SB_HEADS, SB_HEAD_DIM)
    sv = p_sv.reshape(Bsz, L, SB_HEADS, SB_HEAD_DIM)
    if past_k is None:
        k_all, v_all = sk, sv
    else:
        k_all = jnp.concatenate([past_k.astype(sk.dtype), sk], axis=1)
        v_all = jnp.concatenate([past_v.astype(sv.dtype), sv], axis=1)
    sb = stick_breaking_attention(sq, k_all, v_all, k_offset)
    sb = rmsnorm(sb, sb_g).reshape(Bsz, L, SB_W)
    f32 = jnp.float32
    mq = p_mq.reshape(Bsz, L, ML_HEADS, ML_QK_DIM).astype(f32)
    mk = p_mk.reshape(Bsz, L, ML_HEADS, ML_QK_DIM).astype(f32)
    mv = p_mv.reshape(Bsz, L, ML_HEADS, ML_V_DIM).astype(f32)
    logi = soft_cap(p_mi.astype(f32) + b_ig.astype(f32))
    logf = jax.nn.log_sigmoid(soft_cap(p_mf.astype(f32) + b_fg.astype(f32)))
    state0 = (C0.astype(f32), n0.astype(f32), m0.astype(f32))
    (C, n, m), hm = mlstm_sequence(state0, mq, mk, mv, logi, logf, lead)
    ml = (rmsnorm(hm, ml_g).reshape(Bsz, L, ML_W) * jax.nn.sigmoid(p_mo.astype(f32))).astype(h.dtype)
    h = h + jnp.concatenate([sb, ml], axis=-1) @ w_out
    f, conv_new = conv_ffn(rmsnorm(h, n2g), conv0, w_up, cw, w_down)
    h = h + f
    return h, sk, sv, C, n, m, conv_new


def setup_inputs(seed: int = 0) -> dict:
    key = jax.random.key(seed)
    ks = jax.random.split(key, 24)
    f32 = jnp.float32

    def nrm(k, shape, scale):
        return jax.random.normal(k, shape, f32) * scale

    def gain(k, shape):
        return 1.0 + 0.01 * jax.random.normal(k, shape, f32)

    return {
        'x_prompt': nrm(ks[0], (BATCH, SEQ, D_MODEL), 1.0),
        'x_sample': nrm(ks[1], (DEC_BATCH, DEC_SEQ, D_MODEL), 1.0),
        'cache_k': nrm(ks[2], (DEPTH, DEC_BATCH, PAST_LEN, SB_HEADS, SB_HEAD_DIM), 1.0),
        'cache_v': nrm(ks[3], (DEPTH, DEC_BATCH, PAST_LEN, SB_HEADS, SB_HEAD_DIM), 1.0),
        'state_C': nrm(ks[4], (DEPTH, DEC_BATCH, ML_HEADS, ML_QK_DIM, ML_V_DIM), 1.0),
        'state_n': nrm(ks[5], (DEPTH, DEC_BATCH, ML_HEADS, ML_QK_DIM), 1.0),
        'state_m': nrm(ks[6], (DEPTH, DEC_BATCH, ML_HEADS), 1.0),
        'state_conv': nrm(ks[7], (DEPTH, DEC_BATCH, CONV_W - 1, 2 * D_FF), 1.0),
        'meta_tokens': nrm(ks[8], (N_META, D_MODEL), 1.0),
        'norm1_g': gain(ks[9], (DEPTH, D_MODEL)),
        'w_in': nrm(ks[10], (DEPTH, D_MODEL, IN_COLS), D_MODEL ** -0.5),
        'b_igate': nrm(ks[11], (DEPTH, ML_HEADS), 0.01),
        'b_fgate': 3.0 + 3.0 * jax.random.uniform(ks[12], (DEPTH, ML_HEADS), f32),
        'sb_head_g': gain(ks[13], (DEPTH, SB_HEADS, SB_HEAD_DIM)),
        'ml_head_g': gain(ks[14], (DEPTH, ML_HEADS, ML_V_DIM)),
        'w_out': nrm(ks[15], (DEPTH, SB_W + ML_W, D_MODEL), (SB_W + ML_W) ** -0.5),
        'norm2_g': gain(ks[16], (DEPTH, D_MODEL)),
        'w_up': nrm(ks[17], (DEPTH, D_MODEL, 2 * D_FF), D_MODEL ** -0.5),
        'conv_w': nrm(ks[18], (DEPTH, CONV_W, 2 * D_FF), CONV_W ** -0.5),
        'w_down': nrm(ks[19], (DEPTH, D_FF, D_MODEL), D_FF ** -0.5),
        'final_g': gain(ks[20], (D_MODEL,)),
    }


def reference(x_prompt, x_sample, cache_k, cache_v, state_C, state_n, state_m, state_conv,
              meta_tokens, norm1_g, w_in, b_igate, b_fgate, sb_head_g, ml_head_g, w_out,
              norm2_g, w_up, conv_w, w_down, final_g):
    Bp, Sp, _ = x_prompt.shape
    past_len = cache_k.shape[2]
    f32 = jnp.float32
    meta = jnp.broadcast_to(meta_tokens[None].astype(x_prompt.dtype), (Bp, N_META, D_MODEL))
    hp = jnp.concatenate([meta, x_prompt], axis=1)
    hs = x_sample

    pk, pv, pC, pn, pm, pc = [], [], [], [], [], []
    sk_, sv_, sC, sn, sm, sc = [], [], [], [], [], []
    for l in range(DEPTH):
        wts = (norm1_g[l], w_in[l], b_igate[l], b_fgate[l], sb_head_g[l], ml_head_g[l],
               w_out[l], norm2_g[l], w_up[l], conv_w[l], w_down[l])
        zC = jnp.zeros((Bp, ML_HEADS, ML_QK_DIM, ML_V_DIM), f32)
        zn = jnp.zeros((Bp, ML_HEADS, ML_QK_DIM), f32)
        zm = jnp.zeros((Bp, ML_HEADS), f32)
        zconv = jnp.zeros((Bp, CONV_W - 1, 2 * D_FF), hp.dtype)
        hp, k1, v1, C1, n1, m1, c1 = trunk_layer(hp, 0, None, None, zC, zn, zm, zconv, N_META, *wts)
        hs, k2, v2, C2, n2, m2, c2 = trunk_layer(hs, past_len, cache_k[l], cache_v[l],
                                                 state_C[l], state_n[l], state_m[l], state_conv[l], 0, *wts)
        pk.append(k1); pv.append(v1); pC.append(C1); pn.append(n1); pm.append(m1); pc.append(c1)
        sk_.append(k2); sv_.append(v2); sC.append(C2); sn.append(n2); sm.append(m2); sc.append(c2)

    y_prompt = rmsnorm(hp, final_g)[:, N_META:]
    y_sample = rmsnorm(hs, final_g)
    k_prompt, v_prompt = jnp.stack(pk), jnp.stack(pv)
    C_prompt, n_prompt, m_prompt, conv_prompt = jnp.stack(pC), jnp.stack(pn), jnp.stack(pm), jnp.stack(pc)
    k_sample, v_sample = jnp.stack(sk_), jnp.stack(sv_)
    C_sample, n_sample, m_sample, conv_sample = jnp.stack(sC), jnp.stack(sn), jnp.stack(sm), jnp.stack(sc)
    return (y_prompt, y_sample, k_prompt, v_prompt, C_prompt, n_prompt, m_prompt, conv_prompt,
            k_sample, v_sample, C_sample, n_sample, m_sample, conv_sample)
```

```python
import functools

import jax
import jax.numpy as jnp
from jax import lax
from jax.experimental import pallas as pl
from jax.experimental.pallas import tpu as pltpu

F32 = jnp.float32
BF16 = jnp.bfloat16

D_MODEL = 4096
N_META = 16
SB_HEADS = 16
SB_HEAD_DIM = 128
ML_HEADS = 4
ML_QK_DIM = 256
ML_V_DIM = 512
D_FF = 11008
CONV_W = 3
GATE_CAP = 15.0
NORM_EPS = 1e-6
CHUNK = 64

SB_W = SB_HEADS * SB_HEAD_DIM
ML_QK_W = ML_HEADS * ML_QK_DIM
ML_W = ML_HEADS * ML_V_DIM
MAIN_COLS = 3 * SB_W + 2 * ML_QK_W + 2 * ML_W
GATE_PAD = 128

NEG_BIG = -1e30
VMEM_LIMIT_BYTES = 48 * 1024 * 1024

SB_TQ = 128
SB_TK = 256
SB_WIDE = 1024


def _params(*sem):
    return pltpu.CompilerParams(dimension_semantics=sem, vmem_limit_bytes=VMEM_LIMIT_BYTES)


def _split_dot(x, m, dims, passes):
    x_is_lhs, dn = dims
    acc = None
    rem = x
    for _ in range(passes):
        piece = rem.astype(BF16)
        rem = rem - piece.astype(F32)
        ops = (piece, m) if x_is_lhs else (m, piece)
        d = lax.dot_general(ops[0], ops[1], dn, preferred_element_type=F32)
        acc = d if acc is None else acc + d
    return acc


_NN = (((1,), (0,)), ((), ()))
_NT = (((1,), (1,)), ((), ()))
_TN = (((0,), (0,)), ((), ()))


def _softplus(z):
    return jnp.maximum(z, 0.0) + jnp.log(1.0 + jnp.exp(-jnp.abs(z)))


def _rmsnorm_kernel(x_ref, g_ref, o_ref):
    x = x_ref[...]
    ms = jnp.mean(x * x, axis=-1, keepdims=True)
    o_ref[...] = (x * lax.rsqrt(ms + NORM_EPS) * g_ref[...]).astype(o_ref.dtype)


def _rmsnorm(x, g, out_dtype, tm):
    rows, d = x.shape
    return pl.pallas_call(
        _rmsnorm_kernel,
        grid=(pl.cdiv(rows, tm),),
        in_specs=[pl.BlockSpec((tm, d), lambda i: (i, 0)), pl.BlockSpec((1, d), lambda i: (0, 0))],
        out_specs=pl.BlockSpec((tm, d), lambda i: (i, 0)),
        out_shape=jax.ShapeDtypeStruct((rows, d), out_dtype),
        compiler_params=_params("parallel"),
        name="rmsnorm",
    )(x, g.reshape(1, d).astype(F32))


def _mm_kernel(x_ref, w_ref, o_ref):
    o_ref[...] = jnp.dot(x_ref[...], w_ref[...], preferred_element_type=F32).astype(o_ref.dtype)


def _matmul(x, w, tm, tn, out_dtype=F32, name="matmul"):
    rows, k = x.shape
    n = w.shape[1]
    assert n % tn == 0
    return pl.pallas_call(
        _mm_kernel,
        grid=(pl.cdiv(rows, tm), n // tn),
        in_specs=[pl.BlockSpec((tm, k), lambda i, j: (i, 0)), pl.BlockSpec((k, tn), lambda i, j: (0, j))],
        out_specs=pl.BlockSpec((tm, tn), lambda i, j: (i, j)),
        out_shape=jax.ShapeDtypeStruct((rows, n), out_dtype),
        compiler_params=_params("parallel", "arbitrary"),
        name=name,
    )(x, w)


def _mm_res_kernel(x_ref, w_ref, r_ref, o_ref):
    o_ref[...] = r_ref[...] + jnp.dot(x_ref[...], w_ref[...], preferred_element_type=F32)


def _matmul_residual(x, w, res, tm, tn, name):
    rows, k = x.shape
    n = w.shape[1]
    assert n % tn == 0
    return pl.pallas_call(
        _mm_res_kernel,
        grid=(pl.cdiv(rows, tm), n // tn),
        in_specs=[pl.BlockSpec((tm, k), lambda i, j: (i, 0)), pl.BlockSpec((k, tn), lambda i, j: (0, j)),
                  pl.BlockSpec((tm, tn), lambda i, j: (i, j))],
        out_specs=pl.BlockSpec((tm, tn), lambda i, j: (i, j)),
        out_shape=jax.ShapeDtypeStruct((rows, n), F32),
        compiler_params=_params("parallel", "arbitrary"),
        name=name,
    )(x, w, res)


def _mm2_res_kernel(a_ref, b_ref, wa_ref, wb_ref, r_ref, o_ref):
    acc = jnp.dot(a_ref[...], wa_ref[...], preferred_element_type=F32)
    acc = acc + jnp.dot(b_ref[...], wb_ref[...], preferred_element_type=F32)
    o_ref[...] = r_ref[...] + acc


def _out_proj(sb, ml, w_out, res, tm, tn):
    rows = sb.shape[0]
    return pl.pallas_call(
        _mm2_res_kernel,
        grid=(pl.cdiv(rows, tm), D_MODEL // tn),
        in_specs=[pl.BlockSpec((tm, SB_W), lambda i, j: (i, 0)), pl.BlockSpec((tm, ML_W), lambda i, j: (i, 0)),
                  pl.BlockSpec((SB_W, tn), lambda i, j: (0, j)), pl.BlockSpec((ML_W, tn), lambda i, j: (1, j)),
                  pl.BlockSpec((tm, tn), lambda i, j: (i, j))],
        out_specs=pl.BlockSpec((tm, tn), lambda i, j: (i, j)),
        out_shape=jax.ShapeDtypeStruct((rows, D_MODEL), F32),
        compiler_params=_params("parallel", "arbitrary"),
        name="out_proj",
    )(sb, ml, w_out, w_out, res)


def _sb_block(q, k, v, tri, carry, mask):
    z = lax.dot_general(q, k, _NT, preferred_element_type=F32) * (SB_HEAD_DIM ** -0.5)
    ls = -_softplus(z)
    if mask is not None:
        ls = jnp.where(mask, ls, 0.0)
    tk = tri.shape[0]
    nsub = k.shape[0] // tk
    after = [None] * nsub
    for b in reversed(range(nsub)):
        ls_b = ls[:, b * tk:(b + 1) * tk]
        after[b] = _split_dot(ls_b, tri, (True, _NN), 2) + carry
        carry = carry + jnp.sum(ls_b, axis=1, keepdims=True)
    after = after[0] if nsub == 1 else jnp.concatenate(after, axis=1)
    a = jnp.exp(z + ls + after)
    if mask is not None:
        a = jnp.where(mask, a, 0.0)
    o = jnp.dot(a.astype(BF16), v, preferred_element_type=F32)
    return o, carry


def _strict_tri(n):
    j = lax.broadcasted_iota(jnp.int32, (n, n), 0)
    s = lax.broadcasted_iota(jnp.int32, (n, n), 1)
    return jnp.where(j > s, 1.0, 0.0).astype(BF16)


def _head_norm(o, g):
    ms = jnp.mean(o * o, axis=-1, keepdims=True)
    return o * lax.rsqrt(ms + NORM_EPS) * g


def _sb_prompt_kernel(q_ref, k_ref, v_ref, g_ref, o_ref):
    i = pl.program_id(1)
    q = q_ref[...]
    tri = _strict_tri(SB_TK)
    jd = (i * SB_TQ) // SB_WIDE
    qpos = i * SB_TQ + lax.broadcasted_iota(jnp.int32, (SB_TQ, SB_WIDE), 0)
    kpos = jd * SB_WIDE + lax.broadcasted_iota(jnp.int32, (SB_TQ, SB_WIDE), 1)

    def kv(j):
        rows = pl.ds(pl.multiple_of(j * SB_WIDE, SB_WIDE), SB_WIDE)
        return k_ref[rows, :], v_ref[rows, :]

    kd, vd = kv(jd)
    o, carry = _sb_block(q, kd, vd, tri, jnp.zeros((SB_TQ, 1), F32), kpos < qpos)

    def body(t, oc):
        kj, vj = kv(jd - 1 - t)
        o_j, c = _sb_block(q, kj, vj, tri, oc[1], None)
        return oc[0] + o_j, c

    o, carry = lax.fori_loop(0, jd, body, (o, carry))
    o_ref[...] = _head_norm(o, g_ref[...]).astype(o_ref.dtype)


def _sb_prompt(q, kv, g):
    rows = q.shape[0]
    krows = kv.shape[0]
    assert rows % SB_TQ == 0 and krows % SB_WIDE == 0 and krows >= rows
    return pl.pallas_call(
        _sb_prompt_kernel,
        grid=(SB_HEADS, rows // SB_TQ),
        in_specs=[pl.BlockSpec((SB_TQ, SB_HEAD_DIM), lambda h, i: (i, h)),
                  pl.BlockSpec((krows, SB_HEAD_DIM), lambda h, i: (0, h)),
                  pl.BlockSpec((krows, SB_HEAD_DIM), lambda h, i: (0, SB_HEADS + h)),
                  pl.BlockSpec((None, 1, SB_HEAD_DIM), lambda h, i: (h, 0, 0))],
        out_specs=pl.BlockSpec((SB_TQ, SB_HEAD_DIM), lambda h, i: (i, h)),
        out_shape=jax.ShapeDtypeStruct((rows, SB_W), BF16),
        compiler_params=_params("parallel", "arbitrary"),
        name="sb_prompt",
    )(q, kv, kv, g.reshape(SB_HEADS, 1, SB_HEAD_DIM).astype(F32))


def _sb_sample_kernel(q_ref, kn_ref, vn_ref, kc_ref, vc_ref, g_ref, o_ref, acc_ref, carry_ref, *, lq, nsteps):
    s = pl.program_id(1)

    @pl.when(s == 0)
    def _new_keys():
        tri = _strict_tri(lq)
        qi = lax.broadcasted_iota(jnp.int32, (lq, lq), 0)
        ki = lax.broadcasted_iota(jnp.int32, (lq, lq), 1)
        for h in range(SB_HEADS):
            cols = slice(h * SB_HEAD_DIM, (h + 1) * SB_HEAD_DIM)
            o, c = _sb_block(q_ref[:, cols], kn_ref[:, cols], vn_ref[:, cols], tri,
                             jnp.zeros((lq, 1), F32), ki < qi)
            acc_ref[:, cols] = o
            carry_ref[h] = c

    @pl.when(s > 0)
    def _cached_keys():
        tri = _strict_tri(SB_TK)
        for h in range(SB_HEADS):
            cols = slice(h * SB_HEAD_DIM, (h + 1) * SB_HEAD_DIM)
            o, c = _sb_block(q_ref[:, cols], kc_ref[:, cols].astype(BF16), vc_ref[:, cols].astype(BF16), tri,
                             carry_ref[h], None)
            acc_ref[:, cols] += o
            carry_ref[h] = c

    @pl.when(s == nsteps - 1)
    def _finish():
        for h in range(SB_HEADS):
            cols = slice(h * SB_HEAD_DIM, (h + 1) * SB_HEAD_DIM)
            o_ref[:, cols] = _head_norm(acc_ref[:, cols], g_ref[h:h + 1, :]).astype(o_ref.dtype)


def _sb_sample(q, kv, cache_k, cache_v, g, lq):
    nb, past, _ = cache_k.shape
    assert past % SB_TK == 0
    ncb = past // SB_TK
    nsteps = ncb + 1

    def cache_map(b, s):
        return (b, ncb - jnp.maximum(s, 1), 0)

    kern = functools.partial(_sb_sample_kernel, lq=lq, nsteps=nsteps)
    return pl.pallas_call(
        kern,
        grid=(nb, nsteps),
        in_specs=[pl.BlockSpec((lq, SB_W), lambda b, s: (b, 0)),
                  pl.BlockSpec((lq, SB_W), lambda b, s: (b, 0)),
                  pl.BlockSpec((lq, SB_W), lambda b, s: (b, 1)),
                  pl.BlockSpec((None, SB_TK, SB_W), cache_map),
                  pl.BlockSpec((None, SB_TK, SB_W), cache_map),
                  pl.BlockSpec((SB_HEADS, SB_HEAD_DIM), lambda b, s: (0, 0))],
        out_specs=pl.BlockSpec((lq, SB_W), lambda b, s: (b, 0)),
        out_shape=jax.ShapeDtypeStruct((nb * lq, SB_W), BF16),
        scratch_shapes=[pltpu.VMEM((lq, SB_W), F32), pltpu.VMEM((SB_HEADS, lq, 1), F32)],
        compiler_params=_params("parallel", "arbitrary"),
        name="sb_sample",
    )(q, kv, kv, cache_k, cache_v, g.astype(F32))


def _mlstm_kernel(q_ref, k_ref, v_ref, og_ref, gt_ref, gb_ref, c0_ref, n0_ref, m0_ref, mlg_ref,
                  out_ref, c_ref, n_ref, m_ref, *, blk, nvalid):
    c = pl.program_id(1)

    @pl.when(c == 0)
    def _load_state():
        c_ref[...] = c0_ref[...]
        n_ref[...] = n0_ref[...]
        m_ref[...] = m0_ref[...]

    rows = c * blk + lax.broadcasted_iota(jnp.int32, (blk, 1), 0)
    valid = rows < nvalid
    cap = GATE_CAP * jnp.tanh((gt_ref[...] + gb_ref[...]) / GATE_CAP)
    logi = jnp.where(valid, cap, NEG_BIG)
    logf = jnp.where(valid, -_softplus(-cap), 0.0)

    ti = lax.broadcasted_iota(jnp.int32, (blk, blk), 0)
    si = lax.broadcasted_iota(jnp.int32, (blk, blk), 1)
    causal = si <= ti
    tril = jnp.where(causal, 1.0, 0.0).astype(BF16)
    b_all = _split_dot(logf, tril, (False, _NN), 3)

    lane = lax.broadcasted_iota(jnp.int32, (blk, GATE_PAD), 1)
    cols = jnp.where(lane < ML_HEADS, logi, b_all)
    er = lax.broadcasted_iota(jnp.int32, (16, GATE_PAD), 0)
    ec = lax.broadcasted_iota(jnp.int32, (16, GATE_PAD), 1)
    eye = jnp.where(er == ec, 1.0, 0.0).astype(BF16)
    rows_t = _split_dot(cols, eye, (False, _NT), 3)

    for h in range(ML_HEADS):
        li_col = logi[:, h:h + 1]
        b_col = b_all[:, ML_HEADS + h:ML_HEADS + h + 1]
        li_row = rows_t[h:h + 1, :]
        b_row = rows_t[ML_HEADS + h:ML_HEADS + h + 1, :]
        m_prev = m_ref[h][:, 0:1]
        d = jnp.where(causal, b_col - b_row + li_row, NEG_BIG)
        g_col = b_col + m_prev
        m_t = jnp.maximum(g_col, jnp.max(d, axis=1, keepdims=True))
        w_intra = jnp.exp(d - m_t)
        w_inter = jnp.exp(g_col - m_t)

        q = q_ref[:, h * ML_QK_DIM:(h + 1) * ML_QK_DIM] * (ML_QK_DIM ** -0.5)
        k = k_ref[:, h * ML_QK_DIM:(h + 1) * ML_QK_DIM]
        v = v_ref[:, h * ML_V_DIM:(h + 1) * ML_V_DIM]
        qb, kb, vb = q.astype(BF16), k.astype(BF16), v.astype(BF16)
        c_prev = c_ref[h]
        n_prev = n_ref[h]

        s_mat = lax.dot_general(qb, kb, _NT, preferred_element_type=F32) * w_intra
        num = jnp.dot(s_mat.astype(BF16), vb, preferred_element_type=F32)
        num = num + w_inter * jnp.dot(qb, c_prev.astype(BF16), preferred_element_type=F32)
        den = jnp.sum(s_mat, axis=1, keepdims=True) + w_inter * jnp.sum(q * n_prev, axis=1, keepdims=True)
        hv = num / jnp.maximum(jnp.abs(den), jnp.exp(-m_t))

        m_new = m_t[blk - 1:blk, :]
        w_end = jnp.exp(b_col[blk - 1:blk, :] - b_col + li_col - m_new)
        decay = jnp.exp(g_col[blk - 1:blk, :] - m_new)
        kw = k * w_end
        c_ref[h] = decay * c_prev + lax.dot_general(kw.astype(BF16), vb, _TN, preferred_element_type=F32)
        n_ref[h] = decay * n_prev + jnp.sum(kw, axis=0, keepdims=True)
        m_ref[h] = jnp.broadcast_to(m_new, (1, GATE_PAD))

        hn = _head_norm(hv, mlg_ref[h:h + 1, :])
        og = og_ref[:, h * ML_V_DIM:(h + 1) * ML_V_DIM]
        out_ref[:, h * ML_V_DIM:(h + 1) * ML_V_DIM] = (hn / (1.0 + jnp.exp(-og))).astype(out_ref.dtype)


def _mlstm(proj, gates, gbias, c0, n0, m0, ml_g, nstreams, blk, nvalid):
    rows = proj.shape[0]
    per_stream = rows // nstreams
    assert per_stream % blk == 0
    nchunks = per_stream // blk
    qk_blk = (3 * SB_W) // ML_QK_W
    v_blk = (3 * SB_W + 2 * ML_QK_W) // ML_W

    def row_map(col):
        return lambda s, c: (s * nchunks + c, col)

    def state_map(s, c):
        return (s, 0, 0, 0)

    kern = functools.partial(_mlstm_kernel, blk=blk, nvalid=nvalid)
    return pl.pallas_call(
        kern,
        grid=(nstreams, nchunks),
        in_specs=[pl.BlockSpec((blk, ML_QK_W), row_map(qk_blk)),
                  pl.BlockSpec((blk, ML_QK_W), row_map(qk_blk + 1)),
                  pl.BlockSpec((blk, ML_W), row_map(v_blk)),
                  pl.BlockSpec((blk, ML_W), row_map(v_blk + 1)),
                  pl.BlockSpec((blk, GATE_PAD), row_map(0)),
                  pl.BlockSpec((1, GATE_PAD), lambda s, c: (0, 0)),
                  pl.BlockSpec((None, ML_HEADS, ML_QK_DIM, ML_V_DIM), state_map),
                  pl.BlockSpec((None, ML_HEADS, 1, ML_QK_DIM), state_map),
                  pl.BlockSpec((None, ML_HEADS, 1, GATE_PAD), state_map),
                  pl.BlockSpec((ML_HEADS, ML_V_DIM), lambda s, c: (0, 0))],
        out_specs=[pl.BlockSpec((blk, ML_W), row_map(0)),
                   pl.BlockSpec((None, ML_HEADS, ML_QK_DIM, ML_V_DIM), state_map),
                   pl.BlockSpec((None, ML_HEADS, 1, ML_QK_DIM), state_map),
                   pl.BlockSpec((None, ML_HEADS, 1, GATE_PAD), state_map)],
        out_shape=[jax.ShapeDtypeStruct((rows, ML_W), BF16),
                   jax.ShapeDtypeStruct((nstreams, ML_HEADS, ML_QK_DIM, ML_V_DIM), F32),
                   jax.ShapeDtypeStruct((nstreams, ML_HEADS, 1, ML_QK_DIM), F32),
                   jax.ShapeDtypeStruct((nstreams, ML_HEADS, 1, GATE_PAD), F32)],
        compiler_params=_params("parallel", "arbitrary"),
        name="mlstm",
    )(proj, proj, proj, proj, gates, gbias, c0, n0, m0, ml_g.astype(F32))


CONV_HALO = 8


def _conv_gate(ext_g, ext_v, cwg_ref, cwv_ref, nrows, axis):
    def conv(ext, cw_ref):
        acc = None
        for j in range(CONV_W):
            off = CONV_HALO - (CONV_W - 1) + j
            win = ext[:, off:off + nrows, :] if axis == 1 else ext[off:off + nrows, :]
            term = win * cw_ref[j:j + 1, :]
            acc = term if acc is None else acc + term
        return acc

    cg = conv(ext_g, cwg_ref)
    cv = conv(ext_v, cwv_ref)
    return cg / (1.0 + jnp.exp(-cg)) * cv


def _ffn_up_prompt_kernel(x_ref, xh_ref, wg_ref, wv_ref, cwg_ref, cwv_ref, o_ref, eg_ref, ev_ref, *, tm):
    i = pl.program_id(0)
    x = x_ref[...]
    xh = xh_ref[...]
    keep = jnp.where(i > 0, 1.0, 0.0)
    for w_ref, e_ref in ((wg_ref, eg_ref), (wv_ref, ev_ref)):
        w = w_ref[...]
        e_ref[0:CONV_HALO, :] = jnp.dot(xh, w, preferred_element_type=F32) * keep
        e_ref[CONV_HALO:CONV_HALO + tm, :] = jnp.dot(x, w, preferred_element_type=F32)
    o_ref[...] = _conv_gate(eg_ref, ev_ref, cwg_ref, cwv_ref, tm, 0).astype(o_ref.dtype)


def _ffn_up_prompt(xn, w_up, conv_w, tm, tn):
    rows = xn.shape[0]
    assert rows % tm == 0 and tm % CONV_HALO == 0 and D_FF % tn == 0
    nj = D_FF // tn
    hb = tm // CONV_HALO
    kern = functools.partial(_ffn_up_prompt_kernel, tm=tm)
    return pl.pallas_call(
        kern,
        grid=(rows // tm, nj),
        in_specs=[pl.BlockSpec((tm, D_MODEL), lambda i, j: (i, 0)),
                  pl.BlockSpec((CONV_HALO, D_MODEL), lambda i, j: (jnp.maximum(i * hb - 1, 0), 0)),
                  pl.BlockSpec((D_MODEL, tn), lambda i, j: (0, j)),
                  pl.BlockSpec((D_MODEL, tn), lambda i, j: (0, nj + j)),
                  pl.BlockSpec((CONV_W, tn), lambda i, j: (0, j)),
                  pl.BlockSpec((CONV_W, tn), lambda i, j: (0, nj + j))],
        out_specs=pl.BlockSpec((tm, tn), lambda i, j: (i, j)),
        out_shape=jax.ShapeDtypeStruct((rows, D_FF), BF16),
        scratch_shapes=[pltpu.VMEM((CONV_HALO + tm, tn), F32), pltpu.VMEM((CONV_HALO + tm, tn), F32)],
        compiler_params=_params("parallel", "arbitrary"),
        name="ffn_up_prompt",
    )(xn, xn, w_up, w_up, conv_w, conv_w)


def _ffn_up_sample_kernel(x_ref, sg_ref, sv_ref, wg_ref, wv_ref, cwg_ref, cwv_ref, o_ref, eg_ref, ev_ref,
                          *, nb, lq):
    x = x_ref[...]
    lo = CONV_HALO - (CONV_W - 1)
    for w_ref, s_ref, e_ref in ((wg_ref, sg_ref, eg_ref), (wv_ref, sv_ref, ev_ref)):
        u = jnp.dot(x, w_ref[...], preferred_element_type=F32)
        e_ref[:, lo:CONV_HALO, :] = s_ref[...]
        e_ref[:, CONV_HALO:CONV_HALO + lq, :] = u.reshape(nb, lq, u.shape[-1])
    act = _conv_gate(eg_ref, ev_ref, cwg_ref, cwv_ref, lq, 1)
    o_ref[...] = act.reshape(nb * lq, act.shape[-1]).astype(o_ref.dtype)


def _ffn_up_sample(xn, state, w_up, conv_w, nb, lq, tn):
    rows = xn.shape[0]
    assert rows == nb * lq and D_FF % tn == 0
    nj = D_FF // tn
    kern = functools.partial(_ffn_up_sample_kernel, nb=nb, lq=lq)
    return pl.pallas_call(
        kern,
        grid=(nj,),
        in_specs=[pl.BlockSpec((rows, D_MODEL), lambda j: (0, 0)),
                  pl.BlockSpec((nb, CONV_W - 1, tn), lambda j: (0, 0, j)),
                  pl.BlockSpec((nb, CONV_W - 1, tn), lambda j: (0, 0, nj + j)),
                  pl.BlockSpec((D_MODEL, tn), lambda j: (0, j)),
                  pl.BlockSpec((D_MODEL, tn), lambda j: (0, nj + j)),
                  pl.BlockSpec((CONV_W, tn), lambda j: (0, j)),
                  pl.BlockSpec((CONV_W, tn), lambda j: (0, nj + j))],
        out_specs=pl.BlockSpec((rows, tn), lambda j: (0, j)),
        out_shape=jax.ShapeDtypeStruct((rows, D_FF), BF16),
        scratch_shapes=[pltpu.VMEM((nb, CONV_HALO + lq, tn), F32), pltpu.VMEM((nb, CONV_HALO + lq, tn), F32)],
        compiler_params=_params("arbitrary"),
        name="ffn_up_sample",
    )(xn, state, state, w_up, w_up, conv_w, conv_w)


def _layer(h, nstreams, nvalid, past_kv, state, wts, tm):
    (n1g, w_main, w_gate, gbias, sb_g, ml_g, w_out, n2g, w_up, conv_w, w_down, final_g) = wts
    rows = h.shape[0]
    per_stream = rows // nstreams
    c0, n0, m0, conv0 = state

    xn = _rmsnorm(h, n1g, BF16, 256)
    proj = _matmul(xn, w_main, tm, 512, name="in_proj")
    gates = _matmul(xn, w_gate, tm, GATE_PAD, name="gate_proj")
    k_new = proj[:, SB_W:2 * SB_W]
    v_new = proj[:, 2 * SB_W:3 * SB_W]
    q_b = proj[:, :SB_W].astype(BF16)
    kv_b = proj[:, SB_W:3 * SB_W].astype(BF16)

    if past_kv is None:
        kv_b = jnp.pad(kv_b, ((0, -rows % SB_WIDE), (0, 0)))
        sb = _sb_prompt(q_b, kv_b, sb_g)
        blk = CHUNK
    else:
        sb = _sb_sample(q_b, kv_b, past_kv[0], past_kv[1], sb_g, per_stream)
        blk = per_stream
    ml, c_new, n_new, m_new = _mlstm(proj, gates, gbias, c0, n0, m0, ml_g, nstreams, blk, nvalid)

    h1 = _out_proj(sb, ml, w_out, h, tm, 512)
    xn2 = _rmsnorm(h1, n2g, BF16, 256)
    if conv0 is None:
        act = _ffn_up_prompt(xn2, w_up, conv_w, tm, 256)
        tail = xn2[nvalid - 8:nvalid]
        conv_new = _matmul(tail, w_up, 8, 512, name="conv_tail")[8 - (CONV_W - 1):][None]
    else:
        act = _ffn_up_sample(xn2, conv0, w_up, conv_w, nstreams, per_stream, 256)
        tail = xn2.reshape(nstreams, per_stream, D_MODEL)[:, per_stream - 8:].reshape(nstreams * 8, D_MODEL)
        conv_new = _matmul(tail, w_up, nstreams * 8, 512, name="conv_tail")
        conv_new = conv_new.reshape(nstreams, 8, 2 * D_FF)[:, 8 - (CONV_W - 1):]
    h2 = _matmul_residual(act, w_down, h1, tm // 2, 256, name="down_proj")
    y = _rmsnorm(h2, final_g, F32, 256)
    return y, k_new, v_new, c_new, n_new[:, :, 0, :], m_new[:, :, 0, 0], conv_new


def kernel(x_prompt, x_sample, cache_k, cache_v, state_C, state_n, state_m, state_conv, meta_tokens, norm1_g,
           w_in, b_igate, b_fgate, sb_head_g, ml_head_g, w_out, norm2_g, w_up, conv_w, w_down, final_g):
    bp, sp, _ = x_prompt.shape
    nb, lq, _ = x_sample.shape
    assert bp == 1 and w_in.shape[0] == 1
    n_prompt = N_META + sp
    rows_p = -(-n_prompt // 1280) * 1280
    assert rows_p % SB_TK == 0 and rows_p % CHUNK == 0

    w_in0 = w_in[0]
    wts = (norm1_g[0],
           w_in0[:, :MAIN_COLS].astype(BF16),
           jnp.pad(w_in0[:, MAIN_COLS:], ((0, 0), (0, GATE_PAD - 2 * ML_HEADS))).astype(BF16),
           jnp.pad(jnp.concatenate([b_igate[0], b_fgate[0]]), (0, GATE_PAD - 2 * ML_HEADS)).reshape(1, GATE_PAD)
           .astype(F32),
           sb_head_g[0], ml_head_g[0], w_out[0].astype(BF16), norm2_g[0], w_up[0].astype(BF16),
           conv_w[0].astype(F32), w_down[0].astype(BF16), final_g)

    hp = jnp.concatenate([meta_tokens.astype(x_prompt.dtype), x_prompt[0],
                          jnp.zeros((rows_p - n_prompt, D_MODEL), x_prompt.dtype)], axis=0)
    zero_state = (jnp.zeros((1, ML_HEADS, ML_QK_DIM, ML_V_DIM), F32),
                  jnp.zeros((1, ML_HEADS, 1, ML_QK_DIM), F32),
                  jnp.zeros((1, ML_HEADS, 1, GATE_PAD), F32), None)
    yp, kp, vp, cp, np_, mp, convp = _layer(hp, 1, n_prompt, None, zero_state, wts, 1280)

    hs = x_sample.reshape(nb * lq, D_MODEL)
    past = (cache_k[0].reshape(nb, -1, SB_W), cache_v[0].reshape(nb, -1, SB_W))
    s_state = (state_C[0], state_n[0][:, :, None, :],
               jnp.broadcast_to(state_m[0][:, :, None, None], (nb, ML_HEADS, 1, GATE_PAD)), state_conv[0])
    ys, ks, vs, cs, ns, ms, convs = _layer(hs, nb, lq, past, s_state, wts, 1024)

    y_prompt = yp[N_META:n_prompt][None]
    y_sample = ys.reshape(nb, lq, D_MODEL)
    k_prompt = kp[:n_prompt].reshape(1, 1, n_prompt, SB_HEADS, SB_HEAD_DIM)
    v_prompt = vp[:n_prompt].reshape(1, 1, n_prompt, SB_HEADS, SB_HEAD_DIM)
    k_sample = ks.reshape(1, nb, lq, SB_HEADS, SB_HEAD_DIM)
    v_sample = vs.reshape(1, nb, lq, SB_HEADS, SB_HEAD_DIM)
    return (y_prompt, y_sample, k_prompt, v_prompt, cp[None], np_[None], mp[None], convp[None],
            k_sample, v_sample, cs[None], ns[None], ms[None], convs[None])
```

```python
import functools

import jax
import jax.numpy as jnp
from jax import lax
from jax.experimental import pallas as pl
from jax.experimental.pallas import tpu as pltpu

F32 = jnp.float32
BF16 = jnp.bfloat16

D_MODEL = 4096
N_META = 16
SB_HEADS = 16
SB_HEAD_DIM = 128
ML_HEADS = 4
ML_QK_DIM = 256
ML_V_DIM = 512
D_FF = 11008
CONV_W = 3
GATE_CAP = 15.0
NORM_EPS = 1e-6
CHUNK = 64

SB_W = SB_HEADS * SB_HEAD_DIM
ML_QK_W = ML_HEADS * ML_QK_DIM
ML_W = ML_HEADS * ML_V_DIM
MAIN_COLS = 3 * SB_W + 2 * ML_QK_W + 2 * ML_W
GATE_PAD = 128

NEG_BIG = -1e30
VMEM_LIMIT_BYTES = 48 * 1024 * 1024
VMEM_LIMIT_BIG_BYTES = 56 * 1024 * 1024

SB_TQ = 512
SB_TK = 256
SB_DEAD = -120.0
PROMPT_TM = 1536


def _params(*sem, vmem=VMEM_LIMIT_BYTES):
    return pltpu.CompilerParams(dimension_semantics=sem, vmem_limit_bytes=vmem)


def _split_dot(x, m, dims, passes):
    x_is_lhs, dn = dims
    acc = None
    rem = x
    for _ in range(passes):
        piece = rem.astype(BF16)
        rem = rem - piece.astype(F32)
        ops = (piece, m) if x_is_lhs else (m, piece)
        d = lax.dot_general(ops[0], ops[1], dn, preferred_element_type=F32)
        acc = d if acc is None else acc + d
    return acc


_NN = (((1,), (0,)), ((), ()))
_NT = (((1,), (1,)), ((), ()))
_TN = (((0,), (0,)), ((), ()))


def _softplus(z):
    return jnp.maximum(z, 0.0) + jnp.log(1.0 + jnp.exp(-jnp.abs(z)))


def _rmsnorm_rows(x, g):
    ms = jnp.mean(x * x, axis=-1, keepdims=True)
    return x * lax.rsqrt(ms + NORM_EPS) * g


def _rmsnorm_kernel(x_ref, g_ref, o_ref):
    o_ref[...] = _rmsnorm_rows(x_ref[...], g_ref[...]).astype(o_ref.dtype)


def _rmsnorm_skip_kernel(x_ref, nxt_ref, g_ref, o_ref, *, skip):
    x = jnp.concatenate([x_ref[skip:, :], nxt_ref[...]], axis=0)
    o_ref[...] = _rmsnorm_rows(x, g_ref[...]).astype(o_ref.dtype)


def _rmsnorm(x, g, out_dtype, tm, skip=0, rows_out=None):
    rows, d = x.shape
    g2 = g.reshape(1, d).astype(F32)
    if skip == 0:
        assert rows_out in (None, rows)
        return pl.pallas_call(
            _rmsnorm_kernel,
            grid=(pl.cdiv(rows, tm),),
            in_specs=[pl.BlockSpec((tm, d), lambda i: (i, 0)), pl.BlockSpec((1, d), lambda i: (0, 0))],
            out_specs=pl.BlockSpec((tm, d), lambda i: (i, 0)),
            out_shape=jax.ShapeDtypeStruct((rows, d), out_dtype),
            compiler_params=_params("parallel"),
            name="rmsnorm",
        )(x, g2)
    assert tm % skip == 0 and rows_out % tm == 0 and rows_out + skip <= rows
    per = tm // skip
    return pl.pallas_call(
        functools.partial(_rmsnorm_skip_kernel, skip=skip),
        grid=(rows_out // tm,),
        in_specs=[pl.BlockSpec((tm, d), lambda i: (i, 0)), pl.BlockSpec((skip, d), lambda i: ((i + 1) * per, 0)),
                  pl.BlockSpec((1, d), lambda i: (0, 0))],
        out_specs=pl.BlockSpec((tm, d), lambda i: (i, 0)),
        out_shape=jax.ShapeDtypeStruct((rows_out, d), out_dtype),
        compiler_params=_params("parallel"),
        name="rmsnorm_skip",
    )(x, x, g2)


def _mm_kernel(x_ref, w_ref, o_ref):
    o_ref[...] = jnp.dot(x_ref[...], w_ref[...], preferred_element_type=F32).astype(o_ref.dtype)


def _matmul(x, w, tm, tn, out_dtype=F32, name="matmul"):
    rows, k = x.shape
    n = w.shape[1]
    assert n % tn == 0
    return pl.pallas_call(
        _mm_kernel,
        grid=(pl.cdiv(rows, tm), n // tn),
        in_specs=[pl.BlockSpec((tm, k), lambda i, j: (i, 0)), pl.BlockSpec((k, tn), lambda i, j: (0, j))],
        out_specs=pl.BlockSpec((tm, tn), lambda i, j: (i, j)),
        out_shape=jax.ShapeDtypeStruct((rows, n), out_dtype),
        compiler_params=_params("parallel", "arbitrary"),
        name=name,
    )(x, w)


def _mm_dual_kernel(x_ref, w_ref, o32_ref, o16_ref):
    acc = jnp.dot(x_ref[...], w_ref[...], preferred_element_type=F32)
    o32_ref[...] = acc
    o16_ref[...] = acc.astype(BF16)


def _matmul_dual(x, w, tm, tn, rows32, name):
    rows, k = x.shape
    n = w.shape[1]
    assert n % tn == 0 and pl.cdiv(rows32, tm) == pl.cdiv(rows, tm)
    return pl.pallas_call(
        _mm_dual_kernel,
        grid=(pl.cdiv(rows, tm), n // tn),
        in_specs=[pl.BlockSpec((tm, k), lambda i, j: (i, 0)), pl.BlockSpec((k, tn), lambda i, j: (0, j))],
        out_specs=[pl.BlockSpec((tm, tn), lambda i, j: (i, j)), pl.BlockSpec((tm, tn), lambda i, j: (i, j))],
        out_shape=[jax.ShapeDtypeStruct((rows32, n), F32), jax.ShapeDtypeStruct((rows, n), BF16)],
        compiler_params=_params("parallel", "arbitrary"),
        name=name,
    )(x, w)


def _mm_res_kernel(x_ref, w_ref, r_ref, o_ref):
    o_ref[...] = r_ref[...] + jnp.dot(x_ref[...], w_ref[...], preferred_element_type=F32)


def _matmul_residual(x, w, res, tm, tn, name):
    rows, k = x.shape
    n = w.shape[1]
    assert n % tn == 0
    return pl.pallas_call(
        _mm_res_kernel,
        grid=(pl.cdiv(rows, tm), n // tn),
        in_specs=[pl.BlockSpec((tm, k), lambda i, j: (i, 0)), pl.BlockSpec((k, tn), lambda i, j: (0, j)),
                  pl.BlockSpec((tm, tn), lambda i, j: (i, j))],
        out_specs=pl.BlockSpec((tm, tn), lambda i, j: (i, j)),
        out_shape=jax.ShapeDtypeStruct((rows, n), F32),
        compiler_params=_params("parallel", "arbitrary"),
        name=name,
    )(x, w, res)


def _mm2_res_kernel(a_ref, b_ref, wa_ref, wb_ref, r_ref, o_ref):
    acc = jnp.dot(a_ref[...], wa_ref[...], preferred_element_type=F32)
    acc = acc + jnp.dot(b_ref[...], wb_ref[...], preferred_element_type=F32)
    o_ref[...] = r_ref[...] + acc


def _out_proj(sb, ml, w_out, res, tm, tn):
    rows = sb.shape[0]
    return pl.pallas_call(
        _mm2_res_kernel,
        grid=(pl.cdiv(rows, tm), D_MODEL // tn),
        in_specs=[pl.BlockSpec((tm, SB_W), lambda i, j: (i, 0)), pl.BlockSpec((tm, ML_W), lambda i, j: (i, 0)),
                  pl.BlockSpec((SB_W, tn), lambda i, j: (0, j)), pl.BlockSpec((ML_W, tn), lambda i, j: (1, j)),
                  pl.BlockSpec((tm, tn), lambda i, j: (i, j))],
        out_specs=pl.BlockSpec((tm, tn), lambda i, j: (i, j)),
        out_shape=jax.ShapeDtypeStruct((rows, D_MODEL), F32),
        compiler_params=_params("parallel", "arbitrary"),
        name="out_proj",
    )(sb, ml, w_out, w_out, res)


def _sb_block(q, k, v, tri, carry, mask):
    s = lax.dot_general(q, k, _NT, preferred_element_type=F32)
    z = s * (SB_HEAD_DIM ** -0.5)
    nz = s * -(SB_HEAD_DIM ** -0.5)
    ls = jnp.minimum(nz, 0.0) - jnp.log(1.0 + jnp.exp(jnp.minimum(z, nz)))
    if mask is not None:
        ls = jnp.where(mask, ls, 0.0)
    tk = tri.shape[0]
    nsub = k.shape[0] // tk
    after = [None] * nsub
    for b in reversed(range(nsub)):
        ls_b = ls[:, b * tk:(b + 1) * tk]
        after[b] = _split_dot(ls_b, tri, (True, _NN), 2) + carry
        carry = carry + jnp.sum(ls_b, axis=1, keepdims=True)
    after = after[0] if nsub == 1 else jnp.concatenate(after, axis=1)
    a = jnp.exp(z + ls + after)
    if mask is not None:
        a = jnp.where(mask, a, 0.0)
    o = jnp.dot(a.astype(BF16), v, preferred_element_type=F32)
    return o, carry


def _strict_tri(n):
    j = lax.broadcasted_iota(jnp.int32, (n, n), 0)
    s = lax.broadcasted_iota(jnp.int32, (n, n), 1)
    return jnp.where(j > s, 1.0, 0.0).astype(BF16)


def _head_norm(o, g):
    ms = jnp.mean(o * o, axis=-1, keepdims=True)
    return o * lax.rsqrt(ms + NORM_EPS) * g


def _sb_live(carry):
    return (jnp.max(carry) > SB_DEAD).astype(jnp.int32)


def _sb_prompt_kernel(q_ref, k_ref, v_ref, g_ref, o_ref, acc_ref):
    i = pl.program_id(1)
    q = q_ref[...]
    tri = _strict_tri(SB_TK)
    nmask = SB_TQ // SB_TK
    jlow = i * nmask
    rowi = lax.broadcasted_iota(jnp.int32, (SB_TQ, SB_TK), 0)
    coli = lax.broadcasted_iota(jnp.int32, (SB_TQ, SB_TK), 1)

    def kv(j):
        rows = pl.ds(pl.multiple_of(j * SB_TK, SB_TK), SB_TK)
        return k_ref[rows, :], v_ref[rows, :]

    carry = jnp.zeros((SB_TQ, 1), F32)
    for m in reversed(range(nmask)):
        kj, vj = kv(jlow + m)
        o_j, carry = _sb_block(q, kj, vj, tri, carry, m * SB_TK + coli < rowi)
        if m == nmask - 1:
            acc_ref[...] = o_j
        else:
            acc_ref[...] += o_j

    def cond(st):
        return jnp.logical_and(st[0] < jlow, st[2] > 0)

    def body(st):
        kj, vj = kv(jlow - 1 - st[0])
        o_j, c = _sb_block(q, kj, vj, tri, st[1], None)
        acc_ref[...] += o_j
        return st[0] + 1, c, _sb_live(c)

    lax.while_loop(cond, body, (jnp.int32(0), carry, _sb_live(carry)))
    o_ref[...] = _head_norm(acc_ref[...], g_ref[...]).astype(o_ref.dtype)


def _sb_prompt(q, k, v, g):
    rows = q.shape[0]
    assert rows % SB_TQ == 0 and SB_TQ % SB_TK == 0 and k.shape[0] == rows and v.shape[0] == rows
    return pl.pallas_call(
        _sb_prompt_kernel,
        grid=(SB_HEADS, rows // SB_TQ),
        in_specs=[pl.BlockSpec((SB_TQ, SB_HEAD_DIM), lambda h, i: (i, h)),
                  pl.BlockSpec((rows, SB_HEAD_DIM), lambda h, i: (0, h)),
                  pl.BlockSpec((rows, SB_HEAD_DIM), lambda h, i: (0, h)),
                  pl.BlockSpec((None, 1, SB_HEAD_DIM), lambda h, i: (h, 0, 0))],
        out_specs=pl.BlockSpec((SB_TQ, SB_HEAD_DIM), lambda h, i: (i, h)),
        out_shape=jax.ShapeDtypeStruct((rows, SB_W), BF16),
        scratch_shapes=[pltpu.VMEM((SB_TQ, SB_HEAD_DIM), F32)],
        compiler_params=_params("parallel", "arbitrary"),
        name="sb_prompt",
    )(q, k, v, g.reshape(SB_HEADS, 1, SB_HEAD_DIM).astype(F32))


def _sb_sample_kernel(q_ref, kn_ref, vn_ref, kc_hbm, vc_hbm, g_ref, o_ref, kbuf, vbuf, sem, acc_ref, carry_ref,
                      *, lq, ncb):
    b = pl.program_id(0)
    blk_rows = SB_TK * SB_HEADS

    def head_cols(h):
        return slice(h * SB_HEAD_DIM, (h + 1) * SB_HEAD_DIM)

    tri_new = _strict_tri(lq)
    qi = lax.broadcasted_iota(jnp.int32, (lq, lq), 0)
    ki = lax.broadcasted_iota(jnp.int32, (lq, lq), 1)
    for h in range(SB_HEADS):
        cols = head_cols(h)
        o, c = _sb_block(q_ref[:, cols], kn_ref[:, cols], vn_ref[:, cols], tri_new, jnp.zeros((lq, 1), F32), ki < qi)
        acc_ref[:, cols] = o
        carry_ref[h] = c

    tri = _strict_tri(SB_TK)

    def cond(st):
        return jnp.logical_and(st[0] < ncb, st[1] > 0)

    def body(st):
        row0 = pl.multiple_of((ncb - 1 - st[0]) * blk_rows, blk_rows)
        copy_k = pltpu.make_async_copy(kc_hbm.at[b, pl.ds(row0, blk_rows), :], kbuf, sem.at[0])
        copy_v = pltpu.make_async_copy(vc_hbm.at[b, pl.ds(row0, blk_rows), :], vbuf, sem.at[1])
        copy_k.start()
        copy_v.start()
        copy_k.wait()
        copy_v.wait()
        for h in range(SB_HEADS):
            cols = head_cols(h)
            kh = kbuf[pl.ds(h, SB_TK, stride=SB_HEADS), :].astype(BF16)
            vh = vbuf[pl.ds(h, SB_TK, stride=SB_HEADS), :].astype(BF16)
            o, c = _sb_block(q_ref[:, cols], kh, vh, tri, carry_ref[h], None)
            acc_ref[:, cols] += o
            carry_ref[h] = c
        return st[0] + 1, _sb_live(carry_ref[...])

    lax.while_loop(cond, body, (jnp.int32(0), _sb_live(carry_ref[...])))
    for h in range(SB_HEADS):
        cols = head_cols(h)
        o_ref[:, cols] = _head_norm(acc_ref[:, cols], g_ref[h:h + 1, :]).astype(o_ref.dtype)


def _sb_sample(q, k, v, cache_k, cache_v, g, lq):
    nb, cache_rows, _ = cache_k.shape
    assert cache_rows % (SB_TK * SB_HEADS) == 0
    ncb = cache_rows // (SB_TK * SB_HEADS)
    kern = functools.partial(_sb_sample_kernel, lq=lq, ncb=ncb)
    return pl.pallas_call(
        kern,
        grid=(nb,),
        in_specs=[pl.BlockSpec((lq, SB_W), lambda b: (b, 0)),
                  pl.BlockSpec((lq, SB_W), lambda b: (b, 0)),
                  pl.BlockSpec((lq, SB_W), lambda b: (b, 0)),
                  pl.BlockSpec(memory_space=pl.ANY),
                  pl.BlockSpec(memory_space=pl.ANY),
                  pl.BlockSpec((SB_HEADS, SB_HEAD_DIM), lambda b: (0, 0))],
        out_specs=pl.BlockSpec((lq, SB_W), lambda b: (b, 0)),
        out_shape=jax.ShapeDtypeStruct((nb * lq, SB_W), BF16),
        scratch_shapes=[pltpu.VMEM((SB_TK * SB_HEADS, SB_HEAD_DIM), F32),
                        pltpu.VMEM((SB_TK * SB_HEADS, SB_HEAD_DIM), F32),
                        pltpu.SemaphoreType.DMA((2,)),
                        pltpu.VMEM((lq, SB_W), F32),
                        pltpu.VMEM((SB_HEADS, lq, 1), F32)],
        compiler_params=_params("arbitrary"),
        name="sb_sample",
    )(q, k, v, cache_k, cache_v, g.astype(F32))


def _mlstm_kernel(q_ref, k_ref, v_ref, og_ref, gt_ref, gb_ref, c0_ref, n0_ref, m0_ref, mlg_ref,
                  out_ref, c_ref, n_ref, m_ref, *, blk, nvalid):
    c = pl.program_id(1)

    @pl.when(c == 0)
    def _load_state():
        c_ref[...] = c0_ref[...]
        n_ref[...] = n0_ref[...]
        m_ref[...] = m0_ref[...]

    rows = c * blk + lax.broadcasted_iota(jnp.int32, (blk, 1), 0)
    valid = rows < nvalid
    cap = GATE_CAP * jnp.tanh((gt_ref[...] + gb_ref[...]) / GATE_CAP)
    logi = jnp.where(valid, cap, NEG_BIG)
    logf = jnp.where(valid, -_softplus(-cap), 0.0)

    ti = lax.broadcasted_iota(jnp.int32, (blk, blk), 0)
    si = lax.broadcasted_iota(jnp.int32, (blk, blk), 1)
    causal = si <= ti
    tril = jnp.where(causal, 1.0, 0.0).astype(BF16)
    b_all = _split_dot(logf, tril, (False, _NN), 3)

    lane = lax.broadcasted_iota(jnp.int32, (blk, GATE_PAD), 1)
    cols = jnp.where(lane < ML_HEADS, logi, b_all)
    er = lax.broadcasted_iota(jnp.int32, (16, GATE_PAD), 0)
    ec = lax.broadcasted_iota(jnp.int32, (16, GATE_PAD), 1)
    eye = jnp.where(er == ec, 1.0, 0.0).astype(BF16)
    rows_t = _split_dot(cols, eye, (False, _NT), 3)

    for h in range(ML_HEADS):
        li_col = logi[:, h:h + 1]
        b_col = b_all[:, ML_HEADS + h:ML_HEADS + h + 1]
        li_row = rows_t[h:h + 1, :]
        b_row = rows_t[ML_HEADS + h:ML_HEADS + h + 1, :]
        m_prev = m_ref[h][:, 0:1]
        d = jnp.where(causal, b_col - b_row + li_row, NEG_BIG)
        g_col = b_col + m_prev
        m_t = jnp.maximum(g_col, jnp.max(d, axis=1, keepdims=True))
        w_intra = jnp.exp(d - m_t)
        w_inter = jnp.exp(g_col - m_t)

        q = q_ref[:, h * ML_QK_DIM:(h + 1) * ML_QK_DIM] * (ML_QK_DIM ** -0.5)
        k = k_ref[:, h * ML_QK_DIM:(h + 1) * ML_QK_DIM]
        v = v_ref[:, h * ML_V_DIM:(h + 1) * ML_V_DIM]
        qb, kb, vb = q.astype(BF16), k.astype(BF16), v.astype(BF16)
        c_prev = c_ref[h]
        n_prev = n_ref[h]

        s_mat = lax.dot_general(qb, kb, _NT, preferred_element_type=F32) * w_intra
        num = jnp.dot(s_mat.astype(BF16), vb, preferred_element_type=F32)
        num = num + w_inter * jnp.dot(qb, c_prev.astype(BF16), preferred_element_type=F32)
        den = jnp.sum(s_mat, axis=1, keepdims=True) + w_inter * jnp.sum(q * n_prev, axis=1, keepdims=True)
        hv = num / jnp.maximum(jnp.abs(den), jnp.exp(-m_t))

        m_new = m_t[blk - 1:blk, :]
        w_end = jnp.exp(b_col[blk - 1:blk, :] - b_col + li_col - m_new)
        decay = jnp.exp(g_col[blk - 1:blk, :] - m_new)
        kw = k * w_end
        c_ref[h] = decay * c_prev + lax.dot_general(kw.astype(BF16), vb, _TN, preferred_element_type=F32)
        n_ref[h] = decay * n_prev + jnp.sum(kw, axis=0, keepdims=True)
        m_ref[h] = jnp.broadcast_to(m_new, (1, GATE_PAD))

        hn = _head_norm(hv, mlg_ref[h:h + 1, :])
        og = og_ref[:, h * ML_V_DIM:(h + 1) * ML_V_DIM]
        out_ref[:, h * ML_V_DIM:(h + 1) * ML_V_DIM] = (hn / (1.0 + jnp.exp(-og))).astype(out_ref.dtype)


def _mlstm(proj, gates, gbias, c0, n0, m0, ml_g, nstreams, blk, nvalid):
    rows = proj.shape[0]
    per_stream = rows // nstreams
    assert per_stream % blk == 0
    nchunks = per_stream // blk
    qk_blk = 0
    v_blk = (2 * ML_QK_W) // ML_W

    def row_map(col):
        return lambda s, c: (s * nchunks + c, col)

    def state_map(s, c):
        return (s, 0, 0, 0)

    kern = functools.partial(_mlstm_kernel, blk=blk, nvalid=nvalid)
    return pl.pallas_call(
        kern,
        grid=(nstreams, nchunks),
        in_specs=[pl.BlockSpec((blk, ML_QK_W), row_map(qk_blk)),
                  pl.BlockSpec((blk, ML_QK_W), row_map(qk_blk + 1)),
                  pl.BlockSpec((blk, ML_W), row_map(v_blk)),
                  pl.BlockSpec((blk, ML_W), row_map(v_blk + 1)),
                  pl.BlockSpec((blk, GATE_PAD), row_map(0)),
                  pl.BlockSpec((1, GATE_PAD), lambda s, c: (0, 0)),
                  pl.BlockSpec((None, ML_HEADS, ML_QK_DIM, ML_V_DIM), state_map),
                  pl.BlockSpec((None, ML_HEADS, 1, ML_QK_DIM), state_map),
                  pl.BlockSpec((None, ML_HEADS, 1, GATE_PAD), state_map),
                  pl.BlockSpec((ML_HEADS, ML_V_DIM), lambda s, c: (0, 0))],
        out_specs=[pl.BlockSpec((blk, ML_W), row_map(0)),
                   pl.BlockSpec((None, ML_HEADS, ML_QK_DIM, ML_V_DIM), state_map),
                   pl.BlockSpec((None, ML_HEADS, 1, ML_QK_DIM), state_map),
                   pl.BlockSpec((None, ML_HEADS, 1, GATE_PAD), state_map)],
        out_shape=[jax.ShapeDtypeStruct((rows, ML_W), BF16),
                   jax.ShapeDtypeStruct((nstreams, ML_HEADS, ML_QK_DIM, ML_V_DIM), F32),
                   jax.ShapeDtypeStruct((nstreams, ML_HEADS, 1, ML_QK_DIM), F32),
                   jax.ShapeDtypeStruct((nstreams, ML_HEADS, 1, GATE_PAD), F32)],
        compiler_params=_params("parallel", "arbitrary"),
        name="mlstm",
    )(proj, proj, proj, proj, gates, gbias, c0, n0, m0, ml_g.astype(F32))


CONV_HALO = 8


def _conv_gate(ext_g, ext_v, cwg_ref, cwv_ref, nrows, axis):
    def conv(ext, cw_ref):
        acc = None
        for j in range(CONV_W):
            off = CONV_HALO - (CONV_W - 1) + j
            win = ext[:, off:off + nrows, :] if axis == 1 else ext[off:off + nrows, :]
            term = win * cw_ref[j:j + 1, :]
            acc = term if acc is None else acc + term
        return acc

    cg = conv(ext_g, cwg_ref)
    cv = conv(ext_v, cwv_ref)
    return cg / (1.0 + jnp.exp(-cg)) * cv


def _ffn_up_prompt_kernel(x_ref, wg_ref, wv_ref, cwg_ref, cwv_ref, o_ref, eg_ref, ev_ref, tg_ref, tv_ref, *, tm):
    i = pl.program_id(0)
    j = pl.program_id(1)

    @pl.when(jnp.logical_and(i == 0, j == 0))
    def _zero_history():
        tg_ref[...] = jnp.zeros(tg_ref.shape, F32)
        tv_ref[...] = jnp.zeros(tv_ref.shape, F32)

    x = x_ref[...]
    for w_ref, e_ref, t_ref in ((wg_ref, eg_ref, tg_ref), (wv_ref, ev_ref, tv_ref)):
        e_ref[0:CONV_HALO, :] = t_ref[j]
        e_ref[CONV_HALO:CONV_HALO + tm, :] = jnp.dot(x, w_ref[...], preferred_element_type=F32)
        t_ref[j] = e_ref[tm:tm + CONV_HALO, :]
    o_ref[...] = _conv_gate(eg_ref, ev_ref, cwg_ref, cwv_ref, tm, 0).astype(o_ref.dtype)


def _ffn_up_prompt(xn, w_up, conv_w, tm, tn):
    rows = xn.shape[0]
    assert rows % tm == 0 and tm % CONV_HALO == 0 and D_FF % tn == 0
    nj = D_FF // tn
    kern = functools.partial(_ffn_up_prompt_kernel, tm=tm)
    return pl.pallas_call(
        kern,
        grid=(rows // tm, nj),
        in_specs=[pl.BlockSpec((tm, D_MODEL), lambda i, j: (i, 0)),
                  pl.BlockSpec((D_MODEL, tn), lambda i, j: (0, j)),
                  pl.BlockSpec((D_MODEL, tn), lambda i, j: (0, nj + j)),
                  pl.BlockSpec((CONV_W, tn), lambda i, j: (0, j)),
                  pl.BlockSpec((CONV_W, tn), lambda i, j: (0, nj + j))],
        out_specs=pl.BlockSpec((tm, tn), lambda i, j: (i, j)),
        out_shape=jax.ShapeDtypeStruct((rows, D_FF), BF16),
        scratch_shapes=[pltpu.VMEM((CONV_HALO + tm, tn), F32), pltpu.VMEM((CONV_HALO + tm, tn), F32),
                        pltpu.VMEM((nj, CONV_HALO, tn), F32), pltpu.VMEM((nj, CONV_HALO, tn), F32)],
        compiler_params=_params("arbitrary", "arbitrary", vmem=VMEM_LIMIT_BIG_BYTES),
        name="ffn_up_prompt",
    )(xn, w_up, w_up, conv_w, conv_w)


def _ffn_up_sample_kernel(x_ref, sg_ref, sv_ref, wg_ref, wv_ref, cwg_ref, cwv_ref, o_ref, eg_ref, ev_ref,
                          *, nb, lq):
    x = x_ref[...]
    lo = CONV_HALO - (CONV_W - 1)
    for w_ref, s_ref, e_ref in ((wg_ref, sg_ref, eg_ref), (wv_ref, sv_ref, ev_ref)):
        u = jnp.dot(x, w_ref[...], preferred_element_type=F32)
        e_ref[:, lo:CONV_HALO, :] = s_ref[...]
        e_ref[:, CONV_HALO:CONV_HALO + lq, :] = u.reshape(nb, lq, u.shape[-1])
    act = _conv_gate(eg_ref, ev_ref, cwg_ref, cwv_ref, lq, 1)
    o_ref[...] = act.reshape(nb * lq, act.shape[-1]).astype(o_ref.dtype)


def _ffn_up_sample(xn, state, w_up, conv_w, nb, lq, tn):
    rows = xn.shape[0]
    assert rows == nb * lq and D_FF % tn == 0
    nj = D_FF // tn
    kern = functools.partial(_ffn_up_sample_kernel, nb=nb, lq=lq)
    return pl.pallas_call(
        kern,
        grid=(nj,),
        in_specs=[pl.BlockSpec((rows, D_MODEL), lambda j: (0, 0)),
                  pl.BlockSpec((nb, CONV_W - 1, tn), lambda j: (0, 0, j)),
                  pl.BlockSpec((nb, CONV_W - 1, tn), lambda j: (0, 0, nj + j)),
                  pl.BlockSpec((D_MODEL, tn), lambda j: (0, j)),
                  pl.BlockSpec((D_MODEL, tn), lambda j: (0, nj + j)),
                  pl.BlockSpec((CONV_W, tn), lambda j: (0, j)),
                  pl.BlockSpec((CONV_W, tn), lambda j: (0, nj + j))],
        out_specs=pl.BlockSpec((rows, tn), lambda j: (0, j)),
        out_shape=jax.ShapeDtypeStruct((rows, D_FF), BF16),
        scratch_shapes=[pltpu.VMEM((nb, CONV_HALO + lq, tn), F32), pltpu.VMEM((nb, CONV_HALO + lq, tn), F32)],
        compiler_params=_params("arbitrary"),
        name="ffn_up_sample",
    )(xn, state, state, w_up, w_up, conv_w, conv_w)


def _layer(h, nstreams, nvalid, past_kv, state, wts, tm, skip):
    (n1g, w_q, w_k, w_v, w_ml, w_gate, gbias, sb_g, ml_g, w_out, n2g, w_up, conv_w, w_down, final_g) = wts
    rows = h.shape[0]
    per_stream = rows // nstreams
    rows_valid = (nstreams - 1) * per_stream + nvalid
    c0, n0, m0, conv0 = state

    xn = _rmsnorm(h, n1g, BF16, 256)
    q_b = _matmul(xn, w_q, tm, 512, out_dtype=BF16, name="q_proj")
    k_new, k_b = _matmul_dual(xn, w_k, tm, 512, rows_valid, name="k_proj")
    v_new, v_b = _matmul_dual(xn, w_v, tm, 512, rows_valid, name="v_proj")
    pm = _matmul(xn, w_ml, tm, 512, name="ml_proj")
    gates = _matmul(xn, w_gate, tm, GATE_PAD, name="gate_proj")

    if past_kv is None:
        sb = _sb_prompt(q_b, k_b, v_b, sb_g)
        blk = CHUNK
    else:
        sb = _sb_sample(q_b, k_b, v_b, past_kv[0], past_kv[1], sb_g, per_stream)
        blk = per_stream
    ml, c_new, n_new, m_new = _mlstm(pm, gates, gbias, c0, n0, m0, ml_g, nstreams, blk, nvalid)

    h1 = _out_proj(sb, ml, w_out, h, tm, 512)
    xn2 = _rmsnorm(h1, n2g, BF16, 256)
    if conv0 is None:
        act = _ffn_up_prompt(xn2, w_up, conv_w, tm, 256)
        tail = xn2[nvalid - 8:nvalid]
        conv_new = _matmul(tail, w_up, 8, 512, name="conv_tail")[8 - (CONV_W - 1):][None]
    else:
        act = _ffn_up_sample(xn2, conv0, w_up, conv_w, nstreams, per_stream, 256)
        tail = xn2.reshape(nstreams, per_stream, D_MODEL)[:, per_stream - 8:].reshape(nstreams * 8, D_MODEL)
        conv_new = _matmul(tail, w_up, nstreams * 8, 512, name="conv_tail")
        conv_new = conv_new.reshape(nstreams, 8, 2 * D_FF)[:, 8 - (CONV_W - 1):]
    h2 = _matmul_residual(act, w_down, h1, 512, 256, name="down_proj")
    y = _rmsnorm(h2, final_g, F32, 256, skip=skip, rows_out=nstreams * nvalid - skip)
    return y, k_new, v_new, c_new, n_new[:, :, 0, :], m_new[:, :, 0, 0], conv_new


def kernel(x_prompt, x_sample, cache_k, cache_v, state_C, state_n, state_m, state_conv, meta_tokens, norm1_g,
           w_in, b_igate, b_fgate, sb_head_g, ml_head_g, w_out, norm2_g, w_up, conv_w, w_down, final_g):
    bp, sp, _ = x_prompt.shape
    nb, lq, _ = x_sample.shape
    assert bp == 1 and w_in.shape[0] == 1
    n_prompt = N_META + sp
    rows_p = -(-n_prompt // PROMPT_TM) * PROMPT_TM
    assert rows_p % SB_TQ == 0 and rows_p % CHUNK == 0

    w_in0 = w_in[0]
    wts = (norm1_g[0],
           w_in0[:, :SB_W].astype(BF16),
           w_in0[:, SB_W:2 * SB_W].astype(BF16),
           w_in0[:, 2 * SB_W:3 * SB_W].astype(BF16),
           w_in0[:, 3 * SB_W:MAIN_COLS].astype(BF16),
           jnp.pad(w_in0[:, MAIN_COLS:], ((0, 0), (0, GATE_PAD - 2 * ML_HEADS))).astype(BF16),
           jnp.pad(jnp.concatenate([b_igate[0], b_fgate[0]]), (0, GATE_PAD - 2 * ML_HEADS)).reshape(1, GATE_PAD)
           .astype(F32),
           sb_head_g[0], ml_head_g[0], w_out[0].astype(BF16), norm2_g[0], w_up[0].astype(BF16),
           conv_w[0].astype(F32), w_down[0].astype(BF16), final_g)

    hp = jnp.concatenate([meta_tokens.astype(x_prompt.dtype), x_prompt[0],
                          jnp.zeros((rows_p - n_prompt, D_MODEL), x_prompt.dtype)], axis=0)
    zero_state = (jnp.zeros((1, ML_HEADS, ML_QK_DIM, ML_V_DIM), F32),
                  jnp.zeros((1, ML_HEADS, 1, ML_QK_DIM), F32),
                  jnp.zeros((1, ML_HEADS, 1, GATE_PAD), F32), None)
    yp, kp, vp, cp, np_, mp, convp = _layer(hp, 1, n_prompt, None, zero_state, wts, PROMPT_TM, N_META)

    hs = x_sample.reshape(nb * lq, D_MODEL)
    past = (cache_k[0].reshape(nb, -1, SB_HEAD_DIM), cache_v[0].reshape(nb, -1, SB_HEAD_DIM))
    s_state = (state_C[0], state_n[0][:, :, None, :],
               jnp.broadcast_to(state_m[0][:, :, None, None], (nb, ML_HEADS, 1, GATE_PAD)), state_conv[0])
    ys, ks, vs, cs, ns, ms, convs = _layer(hs, nb, lq, past, s_state, wts, 1024, 0)

    y_prompt = yp[None]
    y_sample = ys.reshape(nb, lq, D_MODEL)
    k_prompt = kp.reshape(1, 1, n_prompt, SB_HEADS, SB_HEAD_DIM)
    v_prompt = vp.reshape(1, 1, n_prompt, SB_HEADS, SB_HEAD_DIM)
    k_sample = ks.reshape(1, nb, lq, SB_HEADS, SB_HEAD_DIM)
    v_sample = vs.reshape(1, nb, lq, SB_HEADS, SB_HEAD_DIM)
    return (y_prompt, y_sample, k_prompt, v_prompt, cp[None], np_[None], mp[None], convp[None],
            k_sample, v_sample, cs[None], ns[None], ms[None], convs[None])
```

```python
import functools

import jax
import jax.numpy as jnp
from jax import lax
from jax.experimental import pallas as pl
from jax.experimental.pallas import tpu as pltpu

F32 = jnp.float32
BF16 = jnp.bfloat16

D_MODEL = 4096
N_META = 16
SB_HEADS = 16
SB_HEAD_DIM = 128
ML_HEADS = 4
ML_QK_DIM = 256
ML_V_DIM = 512
D_FF = 11008
CONV_W = 3
GATE_CAP = 15.0
NORM_EPS = 1e-6
CHUNK = 256

SB_W = SB_HEADS * SB_HEAD_DIM
ML_QK_W = ML_HEADS * ML_QK_DIM
ML_W = ML_HEADS * ML_V_DIM
MAIN_COLS = 3 * SB_W + 2 * ML_QK_W + 2 * ML_W
GATE_PAD = 128

NEG_BIG = -1e30
VMEM_LIMIT_BYTES = 48 * 1024 * 1024
VMEM_LIMIT_BIG_BYTES = 56 * 1024 * 1024

SB_TQ = 512
SB_TK = 256
SB_PAIR = 2
SB_DEAD = -120.0
PROMPT_TM = 1536
CONV_HALO = 8
CONV_ROWS = 64
FFN_ROWS = 512


def _params(*sem, vmem=VMEM_LIMIT_BYTES):
    return pltpu.CompilerParams(dimension_semantics=sem, vmem_limit_bytes=vmem)


def _split_dot(x, m, dims, passes):
    x_is_lhs, dn = dims
    acc = None
    rem = x
    for _ in range(passes):
        piece = rem.astype(BF16)
        rem = rem - piece.astype(F32)
        ops = (piece, m) if x_is_lhs else (m, piece)
        d = lax.dot_general(ops[0], ops[1], dn, preferred_element_type=F32)
        acc = d if acc is None else acc + d
    return acc


_NN = (((1,), (0,)), ((), ()))
_NT = (((1,), (1,)), ((), ()))
_TN = (((0,), (0,)), ((), ()))


def _softplus(z):
    return jnp.maximum(z, 0.0) + jnp.log(1.0 + jnp.exp(-jnp.abs(z)))


def _rmsnorm_rows(x, g):
    ms = jnp.mean(x * x, axis=-1, keepdims=True)
    return x * lax.rsqrt(ms + NORM_EPS) * g


def _rmsnorm_kernel(x_ref, g_ref, o_ref):
    o_ref[...] = _rmsnorm_rows(x_ref[...], g_ref[...]).astype(o_ref.dtype)


def _rmsnorm_skip_kernel(x_ref, nxt_ref, g_ref, o_ref, *, skip):
    x = jnp.concatenate([x_ref[skip:, :], nxt_ref[...]], axis=0)
    o_ref[...] = _rmsnorm_rows(x, g_ref[...]).astype(o_ref.dtype)


def _rmsnorm(x, g, out_dtype, tm, skip=0, rows_out=None):
    rows, d = x.shape
    g2 = g.reshape(1, d).astype(F32)
    if skip == 0:
        assert rows_out in (None, rows)
        return pl.pallas_call(
            _rmsnorm_kernel,
            grid=(pl.cdiv(rows, tm),),
            in_specs=[pl.BlockSpec((tm, d), lambda i: (i, 0)), pl.BlockSpec((1, d), lambda i: (0, 0))],
            out_specs=pl.BlockSpec((tm, d), lambda i: (i, 0)),
            out_shape=jax.ShapeDtypeStruct((rows, d), out_dtype),
            compiler_params=_params("parallel"),
            name="rmsnorm",
        )(x, g2)
    assert tm % skip == 0 and rows_out % tm == 0 and rows_out + skip <= rows
    per = tm // skip
    return pl.pallas_call(
        functools.partial(_rmsnorm_skip_kernel, skip=skip),
        grid=(rows_out // tm,),
        in_specs=[pl.BlockSpec((tm, d), lambda i: (i, 0)), pl.BlockSpec((skip, d), lambda i: ((i + 1) * per, 0)),
                  pl.BlockSpec((1, d), lambda i: (0, 0))],
        out_specs=pl.BlockSpec((tm, d), lambda i: (i, 0)),
        out_shape=jax.ShapeDtypeStruct((rows_out, d), out_dtype),
        compiler_params=_params("parallel"),
        name="rmsnorm_skip",
    )(x, x, g2)


def _rmsnorm_prompt_kernel(x_ref, prev_ref, meta_ref, g_ref, h_ref, o_ref, *, nmeta, nvalid):
    i = pl.program_id(0)
    tm = h_ref.shape[0]
    head = jnp.where(i == 0, meta_ref[...], prev_ref[...])
    h = jnp.concatenate([head, x_ref[0:tm - nmeta, :]], axis=0)
    row = i * tm + lax.broadcasted_iota(jnp.int32, (tm, 1), 0)
    h = jnp.where(row < nvalid, h, 0.0)
    h_ref[...] = h
    o_ref[...] = _rmsnorm_rows(h, g_ref[...]).astype(o_ref.dtype)


def _rmsnorm_prompt(x, meta, g, rows_out, tm):
    nx, d = x.shape
    nmeta = meta.shape[0]
    assert tm % nmeta == 0 and nx % tm == 0 and rows_out % tm == 0 and nmeta % 8 == 0
    per = tm // nmeta
    last_x = nx // tm - 1
    last_prev = nx // nmeta - 1
    kern = functools.partial(_rmsnorm_prompt_kernel, nmeta=nmeta, nvalid=nmeta + nx)
    return pl.pallas_call(
        kern,
        grid=(rows_out // tm,),
        in_specs=[pl.BlockSpec((tm, d), lambda i: (jnp.minimum(i, last_x), 0)),
                  pl.BlockSpec((nmeta, d), lambda i: (jnp.clip(i * per - 1, 0, last_prev), 0)),
                  pl.BlockSpec((nmeta, d), lambda i: (0, 0)),
                  pl.BlockSpec((1, d), lambda i: (0, 0))],
        out_specs=[pl.BlockSpec((tm, d), lambda i: (i, 0)), pl.BlockSpec((tm, d), lambda i: (i, 0))],
        out_shape=[jax.ShapeDtypeStruct((rows_out, d), F32), jax.ShapeDtypeStruct((rows_out, d), BF16)],
        compiler_params=_params("parallel"),
        name="rmsnorm_prompt",
    )(x, x, meta.astype(F32), g.reshape(1, d).astype(F32))


def _mm_kernel(x_ref, w_ref, o_ref):
    o_ref[...] = jnp.dot(x_ref[...], w_ref[...], preferred_element_type=F32).astype(o_ref.dtype)


def _matmul(x, w, tm, tn, out_dtype=F32, name="matmul"):
    rows, k = x.shape
    n = w.shape[1]
    assert n % tn == 0
    return pl.pallas_call(
        _mm_kernel,
        grid=(pl.cdiv(rows, tm), n // tn),
        in_specs=[pl.BlockSpec((tm, k), lambda i, j: (i, 0)), pl.BlockSpec((k, tn), lambda i, j: (0, j))],
        out_specs=pl.BlockSpec((tm, tn), lambda i, j: (i, j)),
        out_shape=jax.ShapeDtypeStruct((rows, n), out_dtype),
        compiler_params=_params("parallel", "arbitrary"),
        name=name,
    )(x, w)


def _mm_dual_kernel(x_ref, w_ref, o32_ref, o16_ref):
    acc = jnp.dot(x_ref[...], w_ref[...], preferred_element_type=F32)
    o32_ref[...] = acc
    o16_ref[...] = acc.astype(BF16)


def _matmul_dual(x, w, tm, tn, rows32, name):
    rows, k = x.shape
    n = w.shape[1]
    assert n % tn == 0 and pl.cdiv(rows32, tm) == pl.cdiv(rows, tm)
    return pl.pallas_call(
        _mm_dual_kernel,
        grid=(pl.cdiv(rows, tm), n // tn),
        in_specs=[pl.BlockSpec((tm, k), lambda i, j: (i, 0)), pl.BlockSpec((k, tn), lambda i, j: (0, j))],
        out_specs=[pl.BlockSpec((tm, tn), lambda i, j: (i, j)), pl.BlockSpec((tm, tn), lambda i, j: (i, j))],
        out_shape=[jax.ShapeDtypeStruct((rows32, n), F32), jax.ShapeDtypeStruct((rows, n), BF16)],
        compiler_params=_params("parallel", "arbitrary"),
        name=name,
    )(x, w)


def _mm_res_kernel(x_ref, w_ref, r_ref, o_ref):
    o_ref[...] = r_ref[...] + jnp.dot(x_ref[...], w_ref[...], preferred_element_type=F32)


def _matmul_residual(x, w, res, tm, tn, name, vmem=VMEM_LIMIT_BYTES):
    rows, k = x.shape
    n = w.shape[1]
    assert n % tn == 0
    return pl.pallas_call(
        _mm_res_kernel,
        grid=(pl.cdiv(rows, tm), n // tn),
        in_specs=[pl.BlockSpec((tm, k), lambda i, j: (i, 0)), pl.BlockSpec((k, tn), lambda i, j: (0, j)),
                  pl.BlockSpec((tm, tn), lambda i, j: (i, j))],
        out_specs=pl.BlockSpec((tm, tn), lambda i, j: (i, j)),
        out_shape=jax.ShapeDtypeStruct((rows, n), F32),
        compiler_params=_params("parallel", "arbitrary", vmem=vmem),
        name=name,
    )(x, w, res)


def _mm2_res_kernel(a_ref, b_ref, wa_ref, wb_ref, r_ref, o_ref):
    acc = jnp.dot(a_ref[...], wa_ref[...], preferred_element_type=F32)
    acc = acc + jnp.dot(b_ref[...], wb_ref[...], preferred_element_type=F32)
    o_ref[...] = r_ref[...] + acc


def _out_proj(sb, ml, w_out, res, tm, tn):
    rows = sb.shape[0]
    return pl.pallas_call(
        _mm2_res_kernel,
        grid=(pl.cdiv(rows, tm), D_MODEL // tn),
        in_specs=[pl.BlockSpec((tm, SB_W), lambda i, j: (i, 0)), pl.BlockSpec((tm, ML_W), lambda i, j: (i, 0)),
                  pl.BlockSpec((SB_W, tn), lambda i, j: (0, j)), pl.BlockSpec((ML_W, tn), lambda i, j: (1, j)),
                  pl.BlockSpec((tm, tn), lambda i, j: (i, j))],
        out_specs=pl.BlockSpec((tm, tn), lambda i, j: (i, j)),
        out_shape=jax.ShapeDtypeStruct((rows, D_MODEL), F32),
        compiler_params=_params("parallel", "arbitrary"),
        name="out_proj",
    )(sb, ml, w_out, w_out, res)


def _sb_weights(s, tri, carry, mask):
    z = s * (SB_HEAD_DIM ** -0.5)
    nz = s * -(SB_HEAD_DIM ** -0.5)
    ls = jnp.minimum(nz, 0.0) - jnp.log(1.0 + jnp.exp(jnp.minimum(z, nz)))
    if mask is not None:
        ls = jnp.where(mask, ls, 0.0)
    after = _split_dot(ls, tri, (True, _NN), 2) + carry
    a = jnp.exp(z + ls + after)
    if mask is not None:
        a = jnp.where(mask, a, 0.0)
    return a, carry + jnp.sum(ls, axis=1, keepdims=True)


def _sb_block(q, k, v, tri, carry, mask):
    a, carry = _sb_weights(lax.dot_general(q, k, _NT, preferred_element_type=F32), tri, carry, mask)
    return jnp.dot(a.astype(BF16), v, preferred_element_type=F32), carry


def _strict_tri(n):
    j = lax.broadcasted_iota(jnp.int32, (n, n), 0)
    s = lax.broadcasted_iota(jnp.int32, (n, n), 1)
    return jnp.where(j > s, 1.0, 0.0).astype(BF16)


def _head_norm(o, g):
    ms = jnp.mean(o * o, axis=-1, keepdims=True)
    return o * lax.rsqrt(ms + NORM_EPS) * g


def _sb_live(carry):
    return (jnp.max(carry) > SB_DEAD).astype(jnp.int32)


def _sb_prompt_kernel(q_ref, k_ref, v_ref, g_ref, o_ref, acc_ref):
    i = pl.program_id(1)
    tri = _strict_tri(SB_TK)
    nmask = SB_TQ // SB_TK
    jlow = i * nmask
    rowi = lax.broadcasted_iota(jnp.int32, (SB_TQ, SB_TK), 0)
    coli = lax.broadcasted_iota(jnp.int32, (SB_TQ, SB_TK), 1)
    heads = [slice(p * SB_HEAD_DIM, (p + 1) * SB_HEAD_DIM) for p in range(SB_PAIR)]
    qs = [q_ref[:, c] for c in heads]

    def sweep(j, carries, mask, first):
        rows = pl.ds(pl.multiple_of(j * SB_TK, SB_TK), SB_TK)
        out = []
        for p, c in enumerate(heads):
            o_j, cy = _sb_block(qs[p], k_ref[rows, c], v_ref[rows, c], tri, carries[p], mask)
            if first:
                acc_ref[:, c] = o_j
            else:
                acc_ref[:, c] += o_j
            out.append(cy)
        return tuple(out)

    carries = tuple(jnp.zeros((SB_TQ, 1), F32) for _ in heads)
    for m in reversed(range(nmask)):
        carries = sweep(jlow + m, carries, m * SB_TK + coli < rowi, m == nmask - 1)

    def live(cs):
        return _sb_live(functools.reduce(jnp.maximum, cs))

    def cond(st):
        return jnp.logical_and(st[0] < jlow, st[2] > 0)

    def body(st):
        cs = sweep(jlow - 1 - st[0], st[1], None, False)
        return st[0] + 1, cs, live(cs)

    lax.while_loop(cond, body, (jnp.int32(0), carries, live(carries)))
    for p, c in enumerate(heads):
        o_ref[:, c] = _head_norm(acc_ref[:, c], g_ref[p:p + 1, :]).astype(o_ref.dtype)


def _sb_prompt(q, k, v, g):
    rows = q.shape[0]
    assert rows % SB_TQ == 0 and SB_TQ % SB_TK == 0 and k.shape[0] == rows and v.shape[0] == rows
    width = SB_PAIR * SB_HEAD_DIM
    return pl.pallas_call(
        _sb_prompt_kernel,
        grid=(SB_HEADS // SB_PAIR, rows // SB_TQ),
        in_specs=[pl.BlockSpec((SB_TQ, width), lambda h, i: (i, h)),
                  pl.BlockSpec((rows, width), lambda h, i: (0, h)),
                  pl.BlockSpec((rows, width), lambda h, i: (0, h)),
                  pl.BlockSpec((None, SB_PAIR, SB_HEAD_DIM), lambda h, i: (h, 0, 0))],
        out_specs=pl.BlockSpec((SB_TQ, width), lambda h, i: (i, h)),
        out_shape=jax.ShapeDtypeStruct((rows, SB_W), BF16),
        scratch_shapes=[pltpu.VMEM((SB_TQ, width), F32)],
        compiler_params=_params("parallel", "arbitrary"),
        name="sb_prompt",
    )(q, k, v, g.reshape(SB_HEADS // SB_PAIR, SB_PAIR, SB_HEAD_DIM).astype(F32))


def _sb_sample_kernel(q_ref, kn_ref, vn_ref, kc_hbm, vc_hbm, g_ref, o_ref, kbuf, vbuf, sem, acc_ref, *, lq, ncb):
    b = pl.program_id(0)
    blk_rows = SB_TK * SB_HEADS

    def head_cols(h):
        return slice(h * SB_HEAD_DIM, (h + 1) * SB_HEAD_DIM)

    qs = [q_ref[:, head_cols(h)] for h in range(SB_HEADS)]

    def sweep(keys, vals, tri, carry, mask, first):
        s = jnp.concatenate([lax.dot_general(qs[h], keys(h), _NT, preferred_element_type=F32)
                             for h in range(SB_HEADS)], axis=0)
        a, carry = _sb_weights(s, tri, carry, mask)
        a = a.astype(BF16)
        for h in range(SB_HEADS):
            o = jnp.dot(a[h * lq:(h + 1) * lq], vals(h), preferred_element_type=F32)
            if first:
                acc_ref[:, head_cols(h)] = o
            else:
                acc_ref[:, head_cols(h)] += o
        return carry

    qi = lax.rem(lax.broadcasted_iota(jnp.int32, (SB_HEADS * lq, lq), 0), lq)
    ki = lax.broadcasted_iota(jnp.int32, (SB_HEADS * lq, lq), 1)
    carry = sweep(lambda h: kn_ref[:, head_cols(h)], lambda h: vn_ref[:, head_cols(h)], _strict_tri(lq),
                  jnp.zeros((SB_HEADS * lq, 1), F32), ki < qi, True)
    tri = _strict_tri(SB_TK)

    def cond(st):
        return jnp.logical_and(st[0] < ncb, st[2] > 0)

    def body(st):
        row0 = pl.multiple_of((ncb - 1 - st[0]) * blk_rows, blk_rows)
        copy_k = pltpu.make_async_copy(kc_hbm.at[b, pl.ds(row0, blk_rows), :], kbuf, sem.at[0])
        copy_v = pltpu.make_async_copy(vc_hbm.at[b, pl.ds(row0, blk_rows), :], vbuf, sem.at[1])
        copy_k.start()
        copy_v.start()
        copy_k.wait()
        copy_v.wait()
        c = sweep(lambda h: kbuf[pl.ds(h, SB_TK, stride=SB_HEADS), :].astype(BF16),
                  lambda h: vbuf[pl.ds(h, SB_TK, stride=SB_HEADS), :].astype(BF16), tri, st[1], None, False)
        return st[0] + 1, c, _sb_live(c)

    lax.while_loop(cond, body, (jnp.int32(0), carry, _sb_live(carry)))
    for h in range(SB_HEADS):
        cols = head_cols(h)
        o_ref[:, cols] = _head_norm(acc_ref[:, cols], g_ref[h:h + 1, :]).astype(o_ref.dtype)


def _sb_sample(q, k, v, cache_k, cache_v, g, lq):
    nb, cache_rows, _ = cache_k.shape
    assert cache_rows % (SB_TK * SB_HEADS) == 0
    ncb = cache_rows // (SB_TK * SB_HEADS)
    kern = functools.partial(_sb_sample_kernel, lq=lq, ncb=ncb)
    return pl.pallas_call(
        kern,
        grid=(nb,),
        in_specs=[pl.BlockSpec((lq, SB_W), lambda b: (b, 0)),
                  pl.BlockSpec((lq, SB_W), lambda b: (b, 0)),
                  pl.BlockSpec((lq, SB_W), lambda b: (b, 0)),
                  pl.BlockSpec(memory_space=pl.ANY),
                  pl.BlockSpec(memory_space=pl.ANY),
                  pl.BlockSpec((SB_HEADS, SB_HEAD_DIM), lambda b: (0, 0))],
        out_specs=pl.BlockSpec((lq, SB_W), lambda b: (b, 0)),
        out_shape=jax.ShapeDtypeStruct((nb * lq, SB_W), BF16),
        scratch_shapes=[pltpu.VMEM((SB_TK * SB_HEADS, SB_HEAD_DIM), F32),
                        pltpu.VMEM((SB_TK * SB_HEADS, SB_HEAD_DIM), F32),
                        pltpu.SemaphoreType.DMA((2,)),
                        pltpu.VMEM((lq, SB_W), F32)],
        compiler_params=_params("arbitrary"),
        name="sb_sample",
    )(q, k, v, cache_k, cache_v, g.astype(F32))


def _mlstm_kernel(q_ref, k_ref, v_ref, og_ref, gt_ref, gb_ref, c0_ref, n0_ref, m0_ref, mlg_ref,
                  out_ref, c_ref, n_ref, m_ref, *, blk, nvalid):
    c = pl.program_id(1)

    @pl.when(c == 0)
    def _load_state():
        c_ref[...] = c0_ref[...]
        n_ref[...] = n0_ref[...]
        m_ref[...] = m0_ref[...]

    rows = c * blk + lax.broadcasted_iota(jnp.int32, (blk, 1), 0)
    valid = rows < nvalid
    cap = GATE_CAP * jnp.tanh((gt_ref[...] + gb_ref[...]) / GATE_CAP)
    logi = jnp.where(valid, cap, NEG_BIG)
    logf = jnp.where(valid, -_softplus(-cap), 0.0)

    ti = lax.broadcasted_iota(jnp.int32, (blk, blk), 0)
    si = lax.broadcasted_iota(jnp.int32, (blk, blk), 1)
    causal = si <= ti
    tril = jnp.where(causal, 1.0, 0.0).astype(BF16)
    b_all = _split_dot(logf, tril, (False, _NN), 3)

    lane = lax.broadcasted_iota(jnp.int32, (blk, GATE_PAD), 1)
    cols = jnp.where(lane < ML_HEADS, logi, b_all)
    er = lax.broadcasted_iota(jnp.int32, (16, GATE_PAD), 0)
    ec = lax.broadcasted_iota(jnp.int32, (16, GATE_PAD), 1)
    eye = jnp.where(er == ec, 1.0, 0.0).astype(BF16)
    rows_t = _split_dot(cols, eye, (False, _NT), 3)

    for h in range(ML_HEADS):
        li_col = logi[:, h:h + 1]
        b_col = b_all[:, ML_HEADS + h:ML_HEADS + h + 1]
        li_row = rows_t[h:h + 1, :]
        b_row = rows_t[ML_HEADS + h:ML_HEADS + h + 1, :]
        m_prev = m_ref[h][:, 0:1]
        d = jnp.where(causal, b_col - b_row + li_row, NEG_BIG)
        g_col = b_col + m_prev
        m_t = jnp.maximum(g_col, jnp.max(d, axis=1, keepdims=True))
        w_intra = jnp.exp(d - m_t)
        w_inter = jnp.exp(g_col - m_t)

        q = q_ref[:, h * ML_QK_DIM:(h + 1) * ML_QK_DIM] * (ML_QK_DIM ** -0.5)
        k = k_ref[:, h * ML_QK_DIM:(h + 1) * ML_QK_DIM]
        v = v_ref[:, h * ML_V_DIM:(h + 1) * ML_V_DIM]
        qb, kb, vb = q.astype(BF16), k.astype(BF16), v.astype(BF16)
        c_prev = c_ref[h]
        n_prev = n_ref[h]

        s_mat = lax.dot_general(qb, kb, _NT, preferred_element_type=F32) * w_intra
        num = jnp.dot(s_mat.astype(BF16), vb, preferred_element_type=F32)
        num = num + w_inter * jnp.dot(qb, c_prev.astype(BF16), preferred_element_type=F32)
        den = jnp.sum(s_mat, axis=1, keepdims=True) + w_inter * jnp.sum(q * n_prev, axis=1, keepdims=True)
        hv = num / jnp.maximum(jnp.abs(den), jnp.exp(-m_t))

        m_new = m_t[blk - 1:blk, :]
        w_end = jnp.exp(b_col[blk - 1:blk, :] - b_col + li_col - m_new)
        decay = jnp.exp(g_col[blk - 1:blk, :] - m_new)
        kw = k * w_end
        c_ref[h] = decay * c_prev + lax.dot_general(kw.astype(BF16), vb, _TN, preferred_element_type=F32)
        n_ref[h] = decay * n_prev + jnp.sum(kw, axis=0, keepdims=True)
        m_ref[h] = jnp.broadcast_to(m_new, (1, GATE_PAD))

        hn = _head_norm(hv, mlg_ref[h:h + 1, :])
        og = og_ref[:, h * ML_V_DIM:(h + 1) * ML_V_DIM]
        out_ref[:, h * ML_V_DIM:(h + 1) * ML_V_DIM] = (hn / (1.0 + jnp.exp(-og))).astype(out_ref.dtype)


def _mlstm(proj, gates, gbias, c0, n0, m0, ml_g, nstreams, blk, nvalid):
    rows = proj.shape[0]
    per_stream = rows // nstreams
    assert per_stream % blk == 0
    nchunks = per_stream // blk
    qk_blk = 0
    v_blk = (2 * ML_QK_W) // ML_W

    def row_map(col):
        return lambda s, c: (s * nchunks + c, col)

    def state_map(s, c):
        return (s, 0, 0, 0)

    kern = functools.partial(_mlstm_kernel, blk=blk, nvalid=nvalid)
    return pl.pallas_call(
        kern,
        grid=(nstreams, nchunks),
        in_specs=[pl.BlockSpec((blk, ML_QK_W), row_map(qk_blk)),
                  pl.BlockSpec((blk, ML_QK_W), row_map(qk_blk + 1)),
                  pl.BlockSpec((blk, ML_W), row_map(v_blk)),
                  pl.BlockSpec((blk, ML_W), row_map(v_blk + 1)),
                  pl.BlockSpec((blk, GATE_PAD), row_map(0)),
                  pl.BlockSpec((1, GATE_PAD), lambda s, c: (0, 0)),
                  pl.BlockSpec((None, ML_HEADS, ML_QK_DIM, ML_V_DIM), state_map),
                  pl.BlockSpec((None, ML_HEADS, 1, ML_QK_DIM), state_map),
                  pl.BlockSpec((None, ML_HEADS, 1, GATE_PAD), state_map),
                  pl.BlockSpec((ML_HEADS, ML_V_DIM), lambda s, c: (0, 0))],
        out_specs=[pl.BlockSpec((blk, ML_W), row_map(0)),
                   pl.BlockSpec((None, ML_HEADS, ML_QK_DIM, ML_V_DIM), state_map),
                   pl.BlockSpec((None, ML_HEADS, 1, ML_QK_DIM), state_map),
                   pl.BlockSpec((None, ML_HEADS, 1, GATE_PAD), state_map)],
        out_shape=[jax.ShapeDtypeStruct((rows, ML_W), BF16),
                   jax.ShapeDtypeStruct((nstreams, ML_HEADS, ML_QK_DIM, ML_V_DIM), F32),
                   jax.ShapeDtypeStruct((nstreams, ML_HEADS, 1, ML_QK_DIM), F32),
                   jax.ShapeDtypeStruct((nstreams, ML_HEADS, 1, GATE_PAD), F32)],
        compiler_params=_params("parallel", "arbitrary"),
        name="mlstm",
    )(proj, proj, proj, proj, gates, gbias, c0, n0, m0, ml_g.astype(F32))


def _silu_gate(cg, cv):
    return cg / (1.0 + jnp.exp(-cg)) * cv


def _conv_rows(hist, u, taps):
    ext = jnp.concatenate([hist, u], axis=0)
    acc = u * taps[CONV_W - 1]
    for d in range(1, CONV_W):
        acc = acc + pltpu.roll(ext, d, 0)[CONV_HALO:] * taps[CONV_W - 1 - d]
    return acc


def _ffn_up_prompt_kernel(x_ref, wg_ref, wv_ref, cwg_ref, cwv_ref, o_ref, tg_ref, tv_ref, *, tm):
    i = pl.program_id(0)
    j = pl.program_id(1)

    @pl.when(jnp.logical_and(i == 0, j == 0))
    def _zero_history():
        tg_ref[...] = jnp.zeros(tg_ref.shape, F32)
        tv_ref[...] = jnp.zeros(tv_ref.shape, F32)

    taps_g = [cwg_ref[d:d + 1, :] for d in range(CONV_W)]
    taps_v = [cwv_ref[d:d + 1, :] for d in range(CONV_W)]
    hist_g = tg_ref[j]
    hist_v = tv_ref[j]
    for r0 in range(0, tm, FFN_ROWS):
        x = x_ref[r0:r0 + FFN_ROWS, :]
        ug = jnp.dot(x, wg_ref[...], preferred_element_type=F32)
        uv = jnp.dot(x, wv_ref[...], preferred_element_type=F32)
        act = _silu_gate(_conv_rows(hist_g, ug, taps_g), _conv_rows(hist_v, uv, taps_v))
        o_ref[r0:r0 + FFN_ROWS, :] = act.astype(o_ref.dtype)
        hist_g = ug[FFN_ROWS - CONV_HALO:]
        hist_v = uv[FFN_ROWS - CONV_HALO:]
    tg_ref[j] = hist_g
    tv_ref[j] = hist_v


def _ffn_up_prompt(xn, w_gate, w_val, cw_gate, cw_val, tm, tn):
    rows = xn.shape[0]
    assert rows % tm == 0 and tm % FFN_ROWS == 0
    nj = pl.cdiv(D_FF, tn)
    kern = functools.partial(_ffn_up_prompt_kernel, tm=tm)
    return pl.pallas_call(
        kern,
        grid=(rows // tm, nj),
        in_specs=[pl.BlockSpec((tm, D_MODEL), lambda i, j: (i, 0)),
                  pl.BlockSpec((D_MODEL, tn), lambda i, j: (0, j)),
                  pl.BlockSpec((D_MODEL, tn), lambda i, j: (0, j)),
                  pl.BlockSpec((CONV_W, tn), lambda i, j: (0, j)),
                  pl.BlockSpec((CONV_W, tn), lambda i, j: (0, j))],
        out_specs=pl.BlockSpec((tm, tn), lambda i, j: (i, j)),
        out_shape=jax.ShapeDtypeStruct((rows, D_FF), BF16),
        scratch_shapes=[pltpu.VMEM((nj, CONV_HALO, tn), F32), pltpu.VMEM((nj, CONV_HALO, tn), F32)],
        compiler_params=_params("arbitrary", "arbitrary", vmem=VMEM_LIMIT_BIG_BYTES),
        name="ffn_up_prompt",
    )(xn, w_gate, w_val, cw_gate, cw_val)


def _ffn_up_sample_kernel(x_ref, sg_ref, sv_ref, wg_ref, wv_ref, cwg_ref, cwv_ref, o_ref, eg_ref, ev_ref,
                          *, nb, lq):
    x = x_ref[...]
    lo = CONV_HALO - (CONV_W - 1)
    for w_ref, s_ref, e_ref in ((wg_ref, sg_ref, eg_ref), (wv_ref, sv_ref, ev_ref)):
        u = jnp.dot(x, w_ref[...], preferred_element_type=F32)
        e_ref[:, lo:CONV_HALO, :] = s_ref[...]
        e_ref[:, CONV_HALO:CONV_HALO + lq, :] = u.reshape(nb, lq, u.shape[-1])

    def conv(e_ref, cw_ref, r0, step):
        acc = None
        for d in range(CONV_W):
            term = e_ref[:, lo + d + r0:lo + d + r0 + step, :] * cw_ref[d:d + 1, :]
            acc = term if acc is None else acc + term
        return acc

    step = min(CONV_ROWS, lq)
    for r0 in range(0, lq, step):
        act = _silu_gate(conv(eg_ref, cwg_ref, r0, step), conv(ev_ref, cwv_ref, r0, step))
        for b in range(nb):
            o_ref[b * lq + r0:b * lq + r0 + step, :] = act[b].astype(o_ref.dtype)


def _ffn_up_sample(xn, state, w_gate, w_val, cw_gate, cw_val, nb, lq, tn):
    rows = xn.shape[0]
    assert rows == nb * lq and D_FF % tn == 0 and lq % min(CONV_ROWS, lq) == 0
    nj = D_FF // tn
    kern = functools.partial(_ffn_up_sample_kernel, nb=nb, lq=lq)
    return pl.pallas_call(
        kern,
        grid=(nj,),
        in_specs=[pl.BlockSpec((rows, D_MODEL), lambda j: (0, 0)),
                  pl.BlockSpec((nb, CONV_W - 1, tn), lambda j: (0, 0, j)),
                  pl.BlockSpec((nb, CONV_W - 1, tn), lambda j: (0, 0, nj + j)),
                  pl.BlockSpec((D_MODEL, tn), lambda j: (0, j)),
                  pl.BlockSpec((D_MODEL, tn), lambda j: (0, j)),
                  pl.BlockSpec((CONV_W, tn), lambda j: (0, j)),
                  pl.BlockSpec((CONV_W, tn), lambda j: (0, j))],
        out_specs=pl.BlockSpec((rows, tn), lambda j: (0, j)),
        out_shape=jax.ShapeDtypeStruct((rows, D_FF), BF16),
        scratch_shapes=[pltpu.VMEM((nb, CONV_HALO + lq, tn), F32), pltpu.VMEM((nb, CONV_HALO + lq, tn), F32)],
        compiler_params=_params("arbitrary"),
        name="ffn_up_sample",
    )(xn, state, state, w_gate, w_val, cw_gate, cw_val)


def _layer(h, xn, nstreams, nvalid, past_kv, state, wts, tm, skip):
    (n1g, w_q, w_k, w_v, w_ml, w_gate, gbias, sb_g, ml_g, w_out, n2g, w_upg, w_upv, cw_g, cw_v, w_down,
     final_g) = wts
    rows = h.shape[0]
    per_stream = rows // nstreams
    rows_valid = (nstreams - 1) * per_stream + nvalid
    c0, n0, m0, conv0 = state

    if xn is None:
        xn = _rmsnorm(h, n1g, BF16, 256)
    q_b = _matmul(xn, w_q, tm, 512, out_dtype=BF16, name="q_proj")
    k_new, k_b = _matmul_dual(xn, w_k, tm, 512, rows_valid, name="k_proj")
    v_new, v_b = _matmul_dual(xn, w_v, tm, 512, rows_valid, name="v_proj")
    pm = _matmul(xn, w_ml, tm, 512, name="ml_proj")
    gates = _matmul(xn, w_gate, tm, GATE_PAD, name="gate_proj")

    if past_kv is None:
        sb = _sb_prompt(q_b, k_b, v_b, sb_g)
        blk = CHUNK
    else:
        sb = _sb_sample(q_b, k_b, v_b, past_kv[0], past_kv[1], sb_g, per_stream)
        blk = per_stream
    ml, c_new, n_new, m_new = _mlstm(pm, gates, gbias, c0, n0, m0, ml_g, nstreams, blk, nvalid)

    h1 = _out_proj(sb, ml, w_out, h, tm, 512)
    xn2 = _rmsnorm(h1, n2g, BF16, 256)
    if conv0 is None:
        act = _ffn_up_prompt(xn2, w_upg, w_upv, cw_g, cw_v, tm, 512)
        tail = xn2[nvalid - 8:nvalid]
    else:
        act = _ffn_up_sample(xn2, conv0, w_upg, w_upv, cw_g, cw_v, nstreams, per_stream, 256)
        tail = xn2.reshape(nstreams, per_stream, D_MODEL)[:, per_stream - 8:].reshape(nstreams * 8, D_MODEL)
    conv_new = jnp.concatenate([_matmul(tail, w, tail.shape[0], 256, name="conv_tail") for w in (w_upg, w_upv)],
                               axis=1)
    conv_new = conv_new.reshape(nstreams, 8, 2 * D_FF)[:, 8 - (CONV_W - 1):]
    h2 = _matmul_residual(act, w_down, h1, 768 if rows % 768 == 0 else 512, 256, name="down_proj",
                          vmem=VMEM_LIMIT_BIG_BYTES)
    y = _rmsnorm(h2, final_g, F32, 256, skip=skip, rows_out=nstreams * nvalid - skip)
    return y, k_new, v_new, c_new, n_new[:, :, 0, :], m_new[:, :, 0, 0], conv_new


def kernel(x_prompt, x_sample, cache_k, cache_v, state_C, state_n, state_m, state_conv, meta_tokens, norm1_g,
           w_in, b_igate, b_fgate, sb_head_g, ml_head_g, w_out, norm2_g, w_up, conv_w, w_down, final_g):
    bp, sp, _ = x_prompt.shape
    nb, lq, _ = x_sample.shape
    assert bp == 1 and w_in.shape[0] == 1
    n_prompt = N_META + sp
    rows_p = -(-n_prompt // PROMPT_TM) * PROMPT_TM
    assert rows_p % SB_TQ == 0 and rows_p % CHUNK == 0

    w_in0 = w_in[0]
    wts = (norm1_g[0],
           w_in0[:, :SB_W].astype(BF16),
           w_in0[:, SB_W:2 * SB_W].astype(BF16),
           w_in0[:, 2 * SB_W:3 * SB_W].astype(BF16),
           w_in0[:, 3 * SB_W:MAIN_COLS].astype(BF16),
           jnp.pad(w_in0[:, MAIN_COLS:], ((0, 0), (0, GATE_PAD - 2 * ML_HEADS))).astype(BF16),
           jnp.pad(jnp.concatenate([b_igate[0], b_fgate[0]]), (0, GATE_PAD - 2 * ML_HEADS)).reshape(1, GATE_PAD)
           .astype(F32),
           sb_head_g[0], ml_head_g[0], w_out[0].astype(BF16), norm2_g[0],
           w_up[0][:, :D_FF].astype(BF16), w_up[0][:, D_FF:].astype(BF16),
           conv_w[0][:, :D_FF].astype(F32), conv_w[0][:, D_FF:].astype(F32), w_down[0].astype(BF16), final_g)

    hp, xnp = _rmsnorm_prompt(x_prompt[0], meta_tokens, norm1_g[0], rows_p, 256)
    zero_state = (jnp.zeros((1, ML_HEADS, ML_QK_DIM, ML_V_DIM), F32),
                  jnp.zeros((1, ML_HEADS, 1, ML_QK_DIM), F32),
                  jnp.zeros((1, ML_HEADS, 1, GATE_PAD), F32), None)
    yp, kp, vp, cp, np_, mp, convp = _layer(hp, xnp, 1, n_prompt, None, zero_state, wts, PROMPT_TM, N_META)

    hs = x_sample.reshape(nb * lq, D_MODEL)
    past = (cache_k[0].reshape(nb, -1, SB_HEAD_DIM), cache_v[0].reshape(nb, -1, SB_HEAD_DIM))
    s_state = (state_C[0], state_n[0][:, :, None, :],
               jnp.broadcast_to(state_m[0][:, :, None, None], (nb, ML_HEADS, 1, GATE_PAD)), state_conv[0])
    ys, ks, vs, cs, ns, ms, convs = _layer(hs, None, nb, lq, past, s_state, wts, 1024, 0)

    y_prompt = yp[None]
    y_sample = ys.reshape(nb, lq, D_MODEL)
    k_prompt = kp.reshape(1, 1, n_prompt, SB_HEADS, SB_HEAD_DIM)
    v_prompt = vp.reshape(1, 1, n_prompt, SB_HEADS, SB_HEAD_DIM)
    k_sample = ks.reshape(1, nb, lq, SB_HEADS, SB_HEAD_DIM)
    v_sample = vs.reshape(1, nb, lq, SB_HEADS, SB_HEAD_DIM)
    return (y_prompt, y_sample, k_prompt, v_prompt, cp[None], np_[None], mp[None], convp[None],
            k_sample, v_sample, cs[None], ns[None], ms[None], convs[None])
```

```python
import functools

import jax
import jax.numpy as jnp
from jax import lax
from jax.experimental import pallas as pl
from jax.experimental.pallas import tpu as pltpu

F32 = jnp.float32
BF16 = jnp.bfloat16

D_MODEL = 4096
N_META = 16
SB_HEADS = 16
SB_HEAD_DIM = 128
ML_HEADS = 4
ML_QK_DIM = 256
ML_V_DIM = 512
D_FF = 11008
CONV_W = 3
GATE_CAP = 15.0
NORM_EPS = 1e-6
CHUNK = 256

SB_W = SB_HEADS * SB_HEAD_DIM
ML_QK_W = ML_HEADS * ML_QK_DIM
ML_W = ML_HEADS * ML_V_DIM
MAIN_COLS = 3 * SB_W + 2 * ML_QK_W + 2 * ML_W
GATE_PAD = 128

NEG_BIG = -1e30
VMEM_LIMIT_BYTES = 48 * 1024 * 1024
VMEM_LIMIT_BIG_BYTES = 56 * 1024 * 1024

SB_TQ = 512
SB_TK = 256
SB_PAIR = 2
SB_DEAD = -120.0
PROMPT_TM = 1536
CONV_HALO = 8
CONV_ROWS = 64
FFN_ROWS = 512


def _params(*sem, vmem=VMEM_LIMIT_BYTES):
    return pltpu.CompilerParams(dimension_semantics=sem, vmem_limit_bytes=vmem)


def _split_dot(x, m, dims, passes):
    x_is_lhs, dn = dims
    acc = None
    rem = x
    for _ in range(passes):
        piece = rem.astype(BF16)
        rem = rem - piece.astype(F32)
        ops = (piece, m) if x_is_lhs else (m, piece)
        d = lax.dot_general(ops[0], ops[1], dn, preferred_element_type=F32)
        acc = d if acc is None else acc + d
    return acc


_NN = (((1,), (0,)), ((), ()))
_NT = (((1,), (1,)), ((), ()))
_TN = (((0,), (0,)), ((), ()))


def _softplus(z):
    return jnp.maximum(z, 0.0) + jnp.log(1.0 + jnp.exp(-jnp.abs(z)))


def _rmsnorm_rows(x, g):
    ms = jnp.mean(x * x, axis=-1, keepdims=True)
    return x * lax.rsqrt(ms + NORM_EPS) * g


def _rmsnorm_kernel(x_ref, g_ref, o_ref):
    o_ref[...] = _rmsnorm_rows(x_ref[...], g_ref[...]).astype(o_ref.dtype)


def _rmsnorm_skip_kernel(x_ref, nxt_ref, g_ref, o_ref, *, skip):
    x = jnp.concatenate([x_ref[skip:, :], nxt_ref[...]], axis=0)
    o_ref[...] = _rmsnorm_rows(x, g_ref[...]).astype(o_ref.dtype)


def _rmsnorm(x, g, out_dtype, tm, skip=0, rows_out=None):
    rows, d = x.shape
    g2 = g.reshape(1, d).astype(F32)
    if skip == 0:
        assert rows_out in (None, rows)
        return pl.pallas_call(
            _rmsnorm_kernel,
            grid=(pl.cdiv(rows, tm),),
            in_specs=[pl.BlockSpec((tm, d), lambda i: (i, 0)), pl.BlockSpec((1, d), lambda i: (0, 0))],
            out_specs=pl.BlockSpec((tm, d), lambda i: (i, 0)),
            out_shape=jax.ShapeDtypeStruct((rows, d), out_dtype),
            compiler_params=_params("parallel"),
            name="rmsnorm",
        )(x, g2)
    assert tm % skip == 0 and rows_out % tm == 0 and rows_out + skip <= rows
    per = tm // skip
    return pl.pallas_call(
        functools.partial(_rmsnorm_skip_kernel, skip=skip),
        grid=(rows_out // tm,),
        in_specs=[pl.BlockSpec((tm, d), lambda i: (i, 0)), pl.BlockSpec((skip, d), lambda i: ((i + 1) * per, 0)),
                  pl.BlockSpec((1, d), lambda i: (0, 0))],
        out_specs=pl.BlockSpec((tm, d), lambda i: (i, 0)),
        out_shape=jax.ShapeDtypeStruct((rows_out, d), out_dtype),
        compiler_params=_params("parallel"),
        name="rmsnorm_skip",
    )(x, x, g2)


def _rmsnorm_prompt_kernel(x_ref, prev_ref, meta_ref, g_ref, h_ref, o_ref, *, nmeta, nvalid):
    i = pl.program_id(0)
    tm = h_ref.shape[0]
    head = jnp.where(i == 0, meta_ref[...], prev_ref[...])
    h = jnp.concatenate([head, x_ref[0:tm - nmeta, :]], axis=0)
    row = i * tm + lax.broadcasted_iota(jnp.int32, (tm, 1), 0)
    h = jnp.where(row < nvalid, h, 0.0)
    h_ref[...] = h
    o_ref[...] = _rmsnorm_rows(h, g_ref[...]).astype(o_ref.dtype)


def _rmsnorm_prompt(x, meta, g, rows_out, tm):
    nx, d = x.shape
    nmeta = meta.shape[0]
    assert tm % nmeta == 0 and nx % tm == 0 and rows_out % tm == 0 and nmeta % 8 == 0
    per = tm // nmeta
    last_x = nx // tm - 1
    last_prev = nx // nmeta - 1
    kern = functools.partial(_rmsnorm_prompt_kernel, nmeta=nmeta, nvalid=nmeta + nx)
    return pl.pallas_call(
        kern,
        grid=(rows_out // tm,),
        in_specs=[pl.BlockSpec((tm, d), lambda i: (jnp.minimum(i, last_x), 0)),
                  pl.BlockSpec((nmeta, d), lambda i: (jnp.clip(i * per - 1, 0, last_prev), 0)),
                  pl.BlockSpec((nmeta, d), lambda i: (0, 0)),
                  pl.BlockSpec((1, d), lambda i: (0, 0))],
        out_specs=[pl.BlockSpec((tm, d), lambda i: (i, 0)), pl.BlockSpec((tm, d), lambda i: (i, 0))],
        out_shape=[jax.ShapeDtypeStruct((rows_out, d), F32), jax.ShapeDtypeStruct((rows_out, d), BF16)],
        compiler_params=_params("parallel"),
        name="rmsnorm_prompt",
    )(x, x, meta.astype(F32), g.reshape(1, d).astype(F32))


def _cast_kernel(x_ref, o_ref):
    o_ref[...] = x_ref[...].astype(o_ref.dtype)


def _cast_cols(w, col0, ncols, tk, tn):
    k = w.shape[0]
    assert k % tk == 0 and ncols % tn == 0 and col0 % tn == 0
    off = col0 // tn
    return pl.pallas_call(
        _cast_kernel,
        grid=(k // tk, ncols // tn),
        in_specs=[pl.BlockSpec((tk, tn), lambda i, j: (i, off + j))],
        out_specs=pl.BlockSpec((tk, tn), lambda i, j: (i, j)),
        out_shape=jax.ShapeDtypeStruct((k, ncols), BF16),
        compiler_params=_params("parallel", "parallel"),
        name="cast_cols",
    )(w)


def _cast_tiles(w, tk, tn):
    k, n = w.shape
    assert k % tk == 0 and n % tn == 0
    return pl.pallas_call(
        _cast_kernel,
        grid=(k // tk, n // tn),
        in_specs=[pl.BlockSpec((tk, tn), lambda i, j: (i, j))],
        out_specs=pl.BlockSpec((None, tk, tn), lambda i, j: (j, i, 0)),
        out_shape=jax.ShapeDtypeStruct((n // tn, k, tn), BF16),
        compiler_params=_params("parallel", "parallel"),
        name="cast_tiles",
    )(w)


def _mm_kernel(x_ref, w_ref, o_ref):
    o_ref[...] = jnp.dot(x_ref[...], w_ref[...], preferred_element_type=F32).astype(o_ref.dtype)


def _matmul(x, w, tm, tn, out_dtype=F32, name="matmul"):
    rows, k = x.shape
    n = w.shape[1]
    assert n % tn == 0
    return pl.pallas_call(
        _mm_kernel,
        grid=(pl.cdiv(rows, tm), n // tn),
        in_specs=[pl.BlockSpec((tm, k), lambda i, j: (i, 0)), pl.BlockSpec((k, tn), lambda i, j: (0, j))],
        out_specs=pl.BlockSpec((tm, tn), lambda i, j: (i, j)),
        out_shape=jax.ShapeDtypeStruct((rows, n), out_dtype),
        compiler_params=_params("parallel", "arbitrary"),
        name=name,
    )(x, w)


def _mm_dual_kernel(x_ref, w_ref, o32_ref, o16_ref):
    acc = jnp.dot(x_ref[...], w_ref[...], preferred_element_type=F32)
    o32_ref[...] = acc
    o16_ref[...] = acc.astype(BF16)


def _matmul_dual(x, w, tm, tn, rows32, name):
    rows, k = x.shape
    n = w.shape[1]
    assert n % tn == 0 and pl.cdiv(rows32, tm) == pl.cdiv(rows, tm)
    return pl.pallas_call(
        _mm_dual_kernel,
        grid=(pl.cdiv(rows, tm), n // tn),
        in_specs=[pl.BlockSpec((tm, k), lambda i, j: (i, 0)), pl.BlockSpec((k, tn), lambda i, j: (0, j))],
        out_specs=[pl.BlockSpec((tm, tn), lambda i, j: (i, j)), pl.BlockSpec((tm, tn), lambda i, j: (i, j))],
        out_shape=[jax.ShapeDtypeStruct((rows32, n), F32), jax.ShapeDtypeStruct((rows, n), BF16)],
        compiler_params=_params("parallel", "arbitrary"),
        name=name,
    )(x, w)


def _mm_res_kernel(x_ref, w_ref, r_ref, o_ref):
    o_ref[...] = r_ref[...] + jnp.dot(x_ref[...], w_ref[...], preferred_element_type=F32)


def _matmul_residual(x, w_tiles, res, tm, name, vmem=VMEM_LIMIT_BYTES):
    rows, k = x.shape
    ntiles, _, tn = w_tiles.shape
    n = ntiles * tn
    return pl.pallas_call(
        _mm_res_kernel,
        grid=(pl.cdiv(rows, tm), ntiles),
        in_specs=[pl.BlockSpec((tm, k), lambda i, j: (i, 0)), pl.BlockSpec((None, k, tn), lambda i, j: (j, 0, 0)),
                  pl.BlockSpec((tm, tn), lambda i, j: (i, j))],
        out_specs=pl.BlockSpec((tm, tn), lambda i, j: (i, j)),
        out_shape=jax.ShapeDtypeStruct((rows, n), F32),
        compiler_params=_params("parallel", "arbitrary", vmem=vmem),
        name=name,
    )(x, w_tiles, res)


def _mm2_res_kernel(a_ref, b_ref, wa_ref, wb_ref, r_ref, o_ref):
    acc = jnp.dot(a_ref[...], wa_ref[...], preferred_element_type=F32)
    acc = acc + jnp.dot(b_ref[...], wb_ref[...], preferred_element_type=F32)
    o_ref[...] = r_ref[...] + acc


def _out_proj(sb, ml, w_out, res, tm, tn):
    rows = sb.shape[0]
    return pl.pallas_call(
        _mm2_res_kernel,
        grid=(pl.cdiv(rows, tm), D_MODEL // tn),
        in_specs=[pl.BlockSpec((tm, SB_W), lambda i, j: (i, 0)), pl.BlockSpec((tm, ML_W), lambda i, j: (i, 0)),
                  pl.BlockSpec((SB_W, tn), lambda i, j: (0, j)), pl.BlockSpec((ML_W, tn), lambda i, j: (1, j)),
                  pl.BlockSpec((tm, tn), lambda i, j: (i, j))],
        out_specs=pl.BlockSpec((tm, tn), lambda i, j: (i, j)),
        out_shape=jax.ShapeDtypeStruct((rows, D_MODEL), F32),
        compiler_params=_params("parallel", "arbitrary"),
        name="out_proj",
    )(sb, ml, w_out, w_out, res)


def _sb_weights(s, tri, carry, mask):
    z = s * (SB_HEAD_DIM ** -0.5)
    nz = s * -(SB_HEAD_DIM ** -0.5)
    ls = jnp.minimum(nz, 0.0) - jnp.log(1.0 + jnp.exp(jnp.minimum(z, nz)))
    if mask is not None:
        ls = jnp.where(mask, ls, 0.0)
    after = _split_dot(ls, tri, (True, _NN), 2) + carry
    a = jnp.exp(z + ls + after)
    if mask is not None:
        a = jnp.where(mask, a, 0.0)
    return a, carry + jnp.sum(ls, axis=1, keepdims=True)


def _sb_block(q, k, v, tri, carry, mask):
    a, carry = _sb_weights(lax.dot_general(q, k, _NT, preferred_element_type=F32), tri, carry, mask)
    return jnp.dot(a.astype(BF16), v, preferred_element_type=F32), carry


def _strict_tri(n):
    j = lax.broadcasted_iota(jnp.int32, (n, n), 0)
    s = lax.broadcasted_iota(jnp.int32, (n, n), 1)
    return jnp.where(j > s, 1.0, 0.0).astype(BF16)


def _head_norm(o, g):
    ms = jnp.mean(o * o, axis=-1, keepdims=True)
    return o * lax.rsqrt(ms + NORM_EPS) * g


def _sb_live(carry):
    return (jnp.max(carry) > SB_DEAD).astype(jnp.int32)


def _sb_prompt_kernel(q_ref, k_ref, v_ref, g_ref, o_ref, acc_ref):
    i = pl.program_id(1)
    tri = _strict_tri(SB_TK)
    nmask = SB_TQ // SB_TK
    jlow = i * nmask
    rowi = lax.broadcasted_iota(jnp.int32, (SB_TQ, SB_TK), 0)
    coli = lax.broadcasted_iota(jnp.int32, (SB_TQ, SB_TK), 1)
    heads = [slice(p * SB_HEAD_DIM, (p + 1) * SB_HEAD_DIM) for p in range(SB_PAIR)]
    qs = [q_ref[:, c] for c in heads]

    def sweep(j, carries, mask, first):
        rows = pl.ds(pl.multiple_of(j * SB_TK, SB_TK), SB_TK)
        out = []
        for p, c in enumerate(heads):
            o_j, cy = _sb_block(qs[p], k_ref[rows, c], v_ref[rows, c], tri, carries[p], mask)
            if first:
                acc_ref[:, c] = o_j
            else:
                acc_ref[:, c] += o_j
            out.append(cy)
        return tuple(out)

    carries = tuple(jnp.zeros((SB_TQ, 1), F32) for _ in heads)
    for m in reversed(range(nmask)):
        carries = sweep(jlow + m, carries, m * SB_TK + coli < rowi, m == nmask - 1)

    def live(cs):
        return _sb_live(functools.reduce(jnp.maximum, cs))

    def cond(st):
        return jnp.logical_and(st[0] < jlow, st[2] > 0)

    def body(st):
        cs = sweep(jlow - 1 - st[0], st[1], None, False)
        return st[0] + 1, cs, live(cs)

    lax.while_loop(cond, body, (jnp.int32(0), carries, live(carries)))
    for p, c in enumerate(heads):
        o_ref[:, c] = _head_norm(acc_ref[:, c], g_ref[p:p + 1, :]).astype(o_ref.dtype)


def _sb_prompt(q, k, v, g):
    rows = q.shape[0]
    assert rows % SB_TQ == 0 and SB_TQ % SB_TK == 0 and k.shape[0] == rows and v.shape[0] == rows
    width = SB_PAIR * SB_HEAD_DIM
    return pl.pallas_call(
        _sb_prompt_kernel,
        grid=(SB_HEADS // SB_PAIR, rows // SB_TQ),
        in_specs=[pl.BlockSpec((SB_TQ, width), lambda h, i: (i, h)),
                  pl.BlockSpec((rows, width), lambda h, i: (0, h)),
                  pl.BlockSpec((rows, width), lambda h, i: (0, h)),
                  pl.BlockSpec((None, SB_PAIR, SB_HEAD_DIM), lambda h, i: (h, 0, 0))],
        out_specs=pl.BlockSpec((SB_TQ, width), lambda h, i: (i, h)),
        out_shape=jax.ShapeDtypeStruct((rows, SB_W), BF16),
        scratch_shapes=[pltpu.VMEM((SB_TQ, width), F32)],
        compiler_params=_params("parallel", "arbitrary"),
        name="sb_prompt",
    )(q, k, v, g.reshape(SB_HEADS // SB_PAIR, SB_PAIR, SB_HEAD_DIM).astype(F32))


def _sb_sample_kernel(q_ref, kn_ref, vn_ref, kc_hbm, vc_hbm, g_ref, o_ref, kbuf, vbuf, sem, acc_ref, *, lq, ncb):
    b = pl.program_id(0)
    blk_rows = SB_TK * SB_HEADS

    def head_cols(h):
        return slice(h * SB_HEAD_DIM, (h + 1) * SB_HEAD_DIM)

    qs = [q_ref[:, head_cols(h)] for h in range(SB_HEADS)]

    def sweep(keys, vals, tri, carry, mask, first):
        s = jnp.concatenate([lax.dot_general(qs[h], keys(h), _NT, preferred_element_type=F32)
                             for h in range(SB_HEADS)], axis=0)
        a, carry = _sb_weights(s, tri, carry, mask)
        a = a.astype(BF16)
        for h in range(SB_HEADS):
            o = jnp.dot(a[h * lq:(h + 1) * lq], vals(h), preferred_element_type=F32)
            if first:
                acc_ref[:, head_cols(h)] = o
            else:
                acc_ref[:, head_cols(h)] += o
        return carry

    qi = lax.rem(lax.broadcasted_iota(jnp.int32, (SB_HEADS * lq, lq), 0), lq)
    ki = lax.broadcasted_iota(jnp.int32, (SB_HEADS * lq, lq), 1)
    carry = sweep(lambda h: kn_ref[:, head_cols(h)], lambda h: vn_ref[:, head_cols(h)], _strict_tri(lq),
                  jnp.zeros((SB_HEADS * lq, 1), F32), ki < qi, True)
    tri = _strict_tri(SB_TK)

    def cond(st):
        return jnp.logical_and(st[0] < ncb, st[2] > 0)

    def body(st):
        row0 = pl.multiple_of((ncb - 1 - st[0]) * blk_rows, blk_rows)
        copy_k = pltpu.make_async_copy(kc_hbm.at[b, pl.ds(row0, blk_rows), :], kbuf, sem.at[0])
        copy_v = pltpu.make_async_copy(vc_hbm.at[b, pl.ds(row0, blk_rows), :], vbuf, sem.at[1])
        copy_k.start()
        copy_v.start()
        copy_k.wait()
        copy_v.wait()
        c = sweep(lambda h: kbuf[pl.ds(h, SB_TK, stride=SB_HEADS), :].astype(BF16),
                  lambda h: vbuf[pl.ds(h, SB_TK, stride=SB_HEADS), :].astype(BF16), tri, st[1], None, False)
        return st[0] + 1, c, _sb_live(c)

    lax.while_loop(cond, body, (jnp.int32(0), carry, _sb_live(carry)))
    for h in range(SB_HEADS):
        cols = head_cols(h)
        o_ref[:, cols] = _head_norm(acc_ref[:, cols], g_ref[h:h + 1, :]).astype(o_ref.dtype)


def _sb_sample(q, k, v, cache_k, cache_v, g, lq):
    nb, cache_rows, _ = cache_k.shape
    assert cache_rows % (SB_TK * SB_HEADS) == 0
    ncb = cache_rows // (SB_TK * SB_HEADS)
    kern = functools.partial(_sb_sample_kernel, lq=lq, ncb=ncb)
    return pl.pallas_call(
        kern,
        grid=(nb,),
        in_specs=[pl.BlockSpec((lq, SB_W), lambda b: (b, 0)),
                  pl.BlockSpec((lq, SB_W), lambda b: (b, 0)),
                  pl.BlockSpec((lq, SB_W), lambda b: (b, 0)),
                  pl.BlockSpec(memory_space=pl.ANY),
                  pl.BlockSpec(memory_space=pl.ANY),
                  pl.BlockSpec((SB_HEADS, SB_HEAD_DIM), lambda b: (0, 0))],
        out_specs=pl.BlockSpec((lq, SB_W), lambda b: (b, 0)),
        out_shape=jax.ShapeDtypeStruct((nb * lq, SB_W), BF16),
        scratch_shapes=[pltpu.VMEM((SB_TK * SB_HEADS, SB_HEAD_DIM), F32),
                        pltpu.VMEM((SB_TK * SB_HEADS, SB_HEAD_DIM), F32),
                        pltpu.SemaphoreType.DMA((2,)),
                        pltpu.VMEM((lq, SB_W), F32)],
        compiler_params=_params("arbitrary"),
        name="sb_sample",
    )(q, k, v, cache_k, cache_v, g.astype(F32))


def _mlstm_kernel(q_ref, k_ref, v_ref, og_ref, gt_ref, gb_ref, c0_ref, n0_ref, m0_ref, mlg_ref,
                  out_ref, c_ref, n_ref, m_ref, *, blk, nvalid):
    c = pl.program_id(1)

    @pl.when(c == 0)
    def _load_state():
        c_ref[...] = c0_ref[...]
        n_ref[...] = n0_ref[...]
        m_ref[...] = m0_ref[...]

    rows = c * blk + lax.broadcasted_iota(jnp.int32, (blk, 1), 0)
    valid = rows < nvalid
    cap = GATE_CAP * jnp.tanh((gt_ref[...] + gb_ref[...]) / GATE_CAP)
    logi = jnp.where(valid, cap, NEG_BIG)
    logf = jnp.where(valid, -_softplus(-cap), 0.0)

    ti = lax.broadcasted_iota(jnp.int32, (blk, blk), 0)
    si = lax.broadcasted_iota(jnp.int32, (blk, blk), 1)
    causal = si <= ti
    tril = jnp.where(causal, 1.0, 0.0).astype(BF16)
    b_all = _split_dot(logf, tril, (False, _NN), 3)

    lane = lax.broadcasted_iota(jnp.int32, (blk, GATE_PAD), 1)
    cols = jnp.where(lane < ML_HEADS, logi, b_all)
    er = lax.broadcasted_iota(jnp.int32, (16, GATE_PAD), 0)
    ec = lax.broadcasted_iota(jnp.int32, (16, GATE_PAD), 1)
    eye = jnp.where(er == ec, 1.0, 0.0).astype(BF16)
    rows_t = _split_dot(cols, eye, (False, _NT), 3)

    for h in range(ML_HEADS):
        li_col = logi[:, h:h + 1]
        b_col = b_all[:, ML_HEADS + h:ML_HEADS + h + 1]
        li_row = rows_t[h:h + 1, :]
        b_row = rows_t[ML_HEADS + h:ML_HEADS + h + 1, :]
        m_prev = m_ref[h][:, 0:1]
        d = jnp.where(causal, b_col - b_row + li_row, NEG_BIG)
        g_col = b_col + m_prev
        m_t = jnp.maximum(g_col, jnp.max(d, axis=1, keepdims=True))
        w_intra = jnp.exp(d - m_t)
        w_inter = jnp.exp(g_col - m_t)

        q = q_ref[:, h * ML_QK_DIM:(h + 1) * ML_QK_DIM] * (ML_QK_DIM ** -0.5)
        k = k_ref[:, h * ML_QK_DIM:(h + 1) * ML_QK_DIM]
        v = v_ref[:, h * ML_V_DIM:(h + 1) * ML_V_DIM]
        qb, kb, vb = q.astype(BF16), k.astype(BF16), v.astype(BF16)
        c_prev = c_ref[h]
        n_prev = n_ref[h]

        s_mat = lax.dot_general(qb, kb, _NT, preferred_element_type=F32) * w_intra
        num = jnp.dot(s_mat.astype(BF16), vb, preferred_element_type=F32)
        num = num + w_inter * jnp.dot(qb, c_prev.astype(BF16), preferred_element_type=F32)
        den = jnp.sum(s_mat, axis=1, keepdims=True) + w_inter * jnp.sum(q * n_prev, axis=1, keepdims=True)
        hv = num / jnp.maximum(jnp.abs(den), jnp.exp(-m_t))

        m_new = m_t[blk - 1:blk, :]
        w_end = jnp.exp(b_col[blk - 1:blk, :] - b_col + li_col - m_new)
        decay = jnp.exp(g_col[blk - 1:blk, :] - m_new)
        kw = k * w_end
        c_ref[h] = decay * c_prev + lax.dot_general(kw.astype(BF16), vb, _TN, preferred_element_type=F32)
        n_ref[h] = decay * n_prev + jnp.sum(kw, axis=0, keepdims=True)
        m_ref[h] = jnp.broadcast_to(m_new, (1, GATE_PAD))

        hn = _head_norm(hv, mlg_ref[h:h + 1, :])
        og = og_ref[:, h * ML_V_DIM:(h + 1) * ML_V_DIM]
        out_ref[:, h * ML_V_DIM:(h + 1) * ML_V_DIM] = (hn / (1.0 + jnp.exp(-og))).astype(out_ref.dtype)


def _mlstm(proj, gates, gbias, c0, n0, m0, ml_g, nstreams, blk, nvalid):
    rows = proj.shape[0]
    per_stream = rows // nstreams
    assert per_stream % blk == 0
    nchunks = per_stream // blk
    qk_blk = 0
    v_blk = (2 * ML_QK_W) // ML_W

    def row_map(col):
        return lambda s, c: (s * nchunks + c, col)

    def state_map(s, c):
        return (s, 0, 0, 0)

    kern = functools.partial(_mlstm_kernel, blk=blk, nvalid=nvalid)
    return pl.pallas_call(
        kern,
        grid=(nstreams, nchunks),
        in_specs=[pl.BlockSpec((blk, ML_QK_W), row_map(qk_blk)),
                  pl.BlockSpec((blk, ML_QK_W), row_map(qk_blk + 1)),
                  pl.BlockSpec((blk, ML_W), row_map(v_blk)),
                  pl.BlockSpec((blk, ML_W), row_map(v_blk + 1)),
                  pl.BlockSpec((blk, GATE_PAD), row_map(0)),
                  pl.BlockSpec((1, GATE_PAD), lambda s, c: (0, 0)),
                  pl.BlockSpec((None, ML_HEADS, ML_QK_DIM, ML_V_DIM), state_map),
                  pl.BlockSpec((None, ML_HEADS, 1, ML_QK_DIM), state_map),
                  pl.BlockSpec((None, ML_HEADS, 1, GATE_PAD), state_map),
                  pl.BlockSpec((ML_HEADS, ML_V_DIM), lambda s, c: (0, 0))],
        out_specs=[pl.BlockSpec((blk, ML_W), row_map(0)),
                   pl.BlockSpec((None, ML_HEADS, ML_QK_DIM, ML_V_DIM), state_map),
                   pl.BlockSpec((None, ML_HEADS, 1, ML_QK_DIM), state_map),
                   pl.BlockSpec((None, ML_HEADS, 1, GATE_PAD), state_map)],
        out_shape=[jax.ShapeDtypeStruct((rows, ML_W), BF16),
                   jax.ShapeDtypeStruct((nstreams, ML_HEADS, ML_QK_DIM, ML_V_DIM), F32),
                   jax.ShapeDtypeStruct((nstreams, ML_HEADS, 1, ML_QK_DIM), F32),
                   jax.ShapeDtypeStruct((nstreams, ML_HEADS, 1, GATE_PAD), F32)],
        compiler_params=_params("parallel", "arbitrary"),
        name="mlstm",
    )(proj, proj, proj, proj, gates, gbias, c0, n0, m0, ml_g.astype(F32))


def _silu_gate(cg, cv):
    return cg / (1.0 + jnp.exp(-cg)) * cv


def _conv_rows(hist, u, taps):
    ext = jnp.concatenate([hist, u], axis=0)
    acc = u * taps[CONV_W - 1]
    for d in range(1, CONV_W):
        acc = acc + pltpu.roll(ext, d, 0)[CONV_HALO:] * taps[CONV_W - 1 - d]
    return acc


def _ffn_up_prompt_kernel(x_ref, wg_ref, wv_ref, cwg_ref, cwv_ref, o_ref, sg_ref, sv_ref, tg_ref, tv_ref,
                          *, tm, state_row):
    i = pl.program_id(0)
    j = pl.program_id(1)

    @pl.when(jnp.logical_and(i == 0, j == 0))
    def _zero_history():
        tg_ref[...] = jnp.zeros(tg_ref.shape, F32)
        tv_ref[...] = jnp.zeros(tv_ref.shape, F32)

    taps_g = [cwg_ref[d:d + 1, :] for d in range(CONV_W)]
    taps_v = [cwv_ref[d:d + 1, :] for d in range(CONV_W)]
    hist_g = tg_ref[j]
    hist_v = tv_ref[j]
    for r0 in range(0, tm, FFN_ROWS):
        x = x_ref[r0:r0 + FFN_ROWS, :]
        ug = jnp.dot(x, wg_ref[...], preferred_element_type=F32)
        uv = jnp.dot(x, wv_ref[...], preferred_element_type=F32)
        act = _silu_gate(_conv_rows(hist_g, ug, taps_g), _conv_rows(hist_v, uv, taps_v))
        o_ref[r0:r0 + FFN_ROWS, :] = act.astype(o_ref.dtype)
        hist_g = ug[FFN_ROWS - CONV_HALO:]
        hist_v = uv[FFN_ROWS - CONV_HALO:]
        if r0 <= state_row < r0 + FFN_ROWS:
            sg_ref[...] = ug[state_row - r0:state_row - r0 + CONV_HALO]
            sv_ref[...] = uv[state_row - r0:state_row - r0 + CONV_HALO]
    tg_ref[j] = hist_g
    tv_ref[j] = hist_v


def _ffn_up_prompt(xn, w_gate, w_val, cw_gate, cw_val, tm, tn, nvalid):
    rows = xn.shape[0]
    assert rows % tm == 0 and tm % FFN_ROWS == 0 and nvalid % CONV_HALO == 0
    assert (nvalid - 1) // tm == rows // tm - 1
    nj = pl.cdiv(D_FF, tn)
    kern = functools.partial(_ffn_up_prompt_kernel, tm=tm, state_row=(nvalid - CONV_HALO) % tm)
    return pl.pallas_call(
        kern,
        grid=(rows // tm, nj),
        in_specs=[pl.BlockSpec((tm, D_MODEL), lambda i, j: (i, 0)),
                  pl.BlockSpec((D_MODEL, tn), lambda i, j: (0, j)),
                  pl.BlockSpec((D_MODEL, tn), lambda i, j: (0, j)),
                  pl.BlockSpec((CONV_W, tn), lambda i, j: (0, j)),
                  pl.BlockSpec((CONV_W, tn), lambda i, j: (0, j))],
        out_specs=[pl.BlockSpec((tm, tn), lambda i, j: (i, j)),
                   pl.BlockSpec((None, CONV_HALO, tn), lambda i, j: (i, 0, j)),
                   pl.BlockSpec((None, CONV_HALO, tn), lambda i, j: (i, 0, j))],
        out_shape=[jax.ShapeDtypeStruct((rows, D_FF), BF16),
                   jax.ShapeDtypeStruct((rows // tm, CONV_HALO, D_FF), F32),
                   jax.ShapeDtypeStruct((rows // tm, CONV_HALO, D_FF), F32)],
        scratch_shapes=[pltpu.VMEM((nj, CONV_HALO, tn), F32), pltpu.VMEM((nj, CONV_HALO, tn), F32)],
        compiler_params=_params("arbitrary", "arbitrary", vmem=VMEM_LIMIT_BIG_BYTES),
        name="ffn_up_prompt",
    )(xn, w_gate, w_val, cw_gate, cw_val)


def _ffn_up_sample_kernel(x_ref, sg_ref, sv_ref, wg_ref, wv_ref, cwg_ref, cwv_ref, o_ref, ng_ref, nv_ref,
                          eg_ref, ev_ref, *, nb, lq):
    x = x_ref[...]
    lo = CONV_HALO - (CONV_W - 1)
    for w_ref, s_ref, e_ref, n_ref in ((wg_ref, sg_ref, eg_ref, ng_ref), (wv_ref, sv_ref, ev_ref, nv_ref)):
        u = jnp.dot(x, w_ref[...], preferred_element_type=F32)
        e_ref[:, lo:CONV_HALO, :] = s_ref[...]
        e_ref[:, CONV_HALO:CONV_HALO + lq, :] = u.reshape(nb, lq, u.shape[-1])
        n_ref[...] = e_ref[:, lo + lq:CONV_HALO + lq, :]

    def conv(e_ref, cw_ref, r0, step):
        acc = None
        for d in range(CONV_W):
            term = e_ref[:, lo + d + r0:lo + d + r0 + step, :] * cw_ref[d:d + 1, :]
            acc = term if acc is None else acc + term
        return acc

    step = min(CONV_ROWS, lq)
    for r0 in range(0, lq, step):
        act = _silu_gate(conv(eg_ref, cwg_ref, r0, step), conv(ev_ref, cwv_ref, r0, step))
        for b in range(nb):
            o_ref[b * lq + r0:b * lq + r0 + step, :] = act[b].astype(o_ref.dtype)


def _ffn_up_sample(xn, state, w_gate, w_val, cw_gate, cw_val, nb, lq, tn):
    rows = xn.shape[0]
    assert rows == nb * lq and D_FF % tn == 0 and lq % min(CONV_ROWS, lq) == 0
    nj = D_FF // tn
    kern = functools.partial(_ffn_up_sample_kernel, nb=nb, lq=lq)
    return pl.pallas_call(
        kern,
        grid=(nj,),
        in_specs=[pl.BlockSpec((rows, D_MODEL), lambda j: (0, 0)),
                  pl.BlockSpec((nb, CONV_W - 1, tn), lambda j: (0, 0, j)),
                  pl.BlockSpec((nb, CONV_W - 1, tn), lambda j: (0, 0, nj + j)),
                  pl.BlockSpec((D_MODEL, tn), lambda j: (0, j)),
                  pl.BlockSpec((D_MODEL, tn), lambda j: (0, j)),
                  pl.BlockSpec((CONV_W, tn), lambda j: (0, j)),
                  pl.BlockSpec((CONV_W, tn), lambda j: (0, j))],
        out_specs=[pl.BlockSpec((rows, tn), lambda j: (0, j)),
                   pl.BlockSpec((nb, CONV_W - 1, tn), lambda j: (0, 0, j)),
                   pl.BlockSpec((nb, CONV_W - 1, tn), lambda j: (0, 0, j))],
        out_shape=[jax.ShapeDtypeStruct((rows, D_FF), BF16),
                   jax.ShapeDtypeStruct((nb, CONV_W - 1, D_FF), F32),
                   jax.ShapeDtypeStruct((nb, CONV_W - 1, D_FF), F32)],
        scratch_shapes=[pltpu.VMEM((nb, CONV_HALO + lq, tn), F32), pltpu.VMEM((nb, CONV_HALO + lq, tn), F32)],
        compiler_params=_params("arbitrary"),
        name="ffn_up_sample",
    )(xn, state, state, w_gate, w_val, cw_gate, cw_val)


def _layer(h, xn, nstreams, nvalid, past_kv, state, wts, tm, skip):
    (n1g, w_q, w_k, w_v, w_ml, w_gate, gbias, sb_g, ml_g, w_out, n2g, w_upg, w_upv, cw_g, cw_v, w_down,
     final_g) = wts
    rows = h.shape[0]
    per_stream = rows // nstreams
    rows_valid = (nstreams - 1) * per_stream + nvalid
    c0, n0, m0, conv0 = state

    if xn is None:
        xn = _rmsnorm(h, n1g, BF16, 256)
    q_b = _matmul(xn, w_q, tm, 512, out_dtype=BF16, name="q_proj")
    k_new, k_b = _matmul_dual(xn, w_k, tm, 512, rows_valid, name="k_proj")
    v_new, v_b = _matmul_dual(xn, w_v, tm, 512, rows_valid, name="v_proj")
    pm = _matmul(xn, w_ml, tm, 512, name="ml_proj")
    gates = _matmul(xn, w_gate, tm, GATE_PAD, name="gate_proj")

    if past_kv is None:
        sb = _sb_prompt(q_b, k_b, v_b, sb_g)
        blk = CHUNK
    else:
        sb = _sb_sample(q_b, k_b, v_b, past_kv[0], past_kv[1], sb_g, per_stream)
        blk = per_stream
    ml, c_new, n_new, m_new = _mlstm(pm, gates, gbias, c0, n0, m0, ml_g, nstreams, blk, nvalid)

    h1 = _out_proj(sb, ml, w_out, h, tm, 512)
    xn2 = _rmsnorm(h1, n2g, BF16, 256)
    if conv0 is None:
        act, tail_g, tail_v = _ffn_up_prompt(xn2, w_upg, w_upv, cw_g, cw_v, tm, 512, nvalid)
        conv_new = jnp.concatenate([tail_g[-1], tail_v[-1]], axis=1)[None, CONV_HALO - (CONV_W - 1):]
    else:
        act, new_g, new_v = _ffn_up_sample(xn2, conv0, w_upg, w_upv, cw_g, cw_v, nstreams, per_stream, 256)
        conv_new = jnp.concatenate([new_g, new_v], axis=2)
    h2 = _matmul_residual(act, w_down, h1, 768 if rows % 768 == 0 else 512, name="down_proj",
                          vmem=VMEM_LIMIT_BIG_BYTES)
    y = _rmsnorm(h2, final_g, F32, 256, skip=skip, rows_out=nstreams * nvalid - skip)
    return y, k_new, v_new, c_new, n_new[:, :, 0, :], m_new[:, :, 0, 0], conv_new


def kernel(x_prompt, x_sample, cache_k, cache_v, state_C, state_n, state_m, state_conv, meta_tokens, norm1_g,
           w_in, b_igate, b_fgate, sb_head_g, ml_head_g, w_out, norm2_g, w_up, conv_w, w_down, final_g):
    bp, sp, _ = x_prompt.shape
    nb, lq, _ = x_sample.shape
    assert bp == 1 and w_in.shape[0] == 1
    n_prompt = N_META + sp
    rows_p = -(-n_prompt // PROMPT_TM) * PROMPT_TM
    assert rows_p % SB_TQ == 0 and rows_p % CHUNK == 0

    w_in0 = w_in[0]
    wts = (norm1_g[0],
           _cast_cols(w_in0, 0, SB_W, 512, SB_W),
           _cast_cols(w_in0, SB_W, SB_W, 512, SB_W),
           _cast_cols(w_in0, 2 * SB_W, SB_W, 512, SB_W),
           _cast_cols(w_in0, 3 * SB_W, MAIN_COLS - 3 * SB_W, 512, SB_W),
           jnp.pad(w_in0[:, MAIN_COLS:], ((0, 0), (0, GATE_PAD - 2 * ML_HEADS))).astype(BF16),
           jnp.pad(jnp.concatenate([b_igate[0], b_fgate[0]]), (0, GATE_PAD - 2 * ML_HEADS)).reshape(1, GATE_PAD)
           .astype(F32),
           sb_head_g[0], ml_head_g[0], w_out[0].astype(BF16), norm2_g[0],
           _cast_cols(w_up[0], 0, D_FF, 128, D_FF), _cast_cols(w_up[0], D_FF, D_FF, 128, D_FF),
           conv_w[0][:, :D_FF].astype(F32), conv_w[0][:, D_FF:].astype(F32),
           _cast_tiles(w_down[0], D_FF // 4, 256), final_g)

    hp, xnp = _rmsnorm_prompt(x_prompt[0], meta_tokens, norm1_g[0], rows_p, 256)
    zero_state = (jnp.zeros((1, ML_HEADS, ML_QK_DIM, ML_V_DIM), F32),
                  jnp.zeros((1, ML_HEADS, 1, ML_QK_DIM), F32),
                  jnp.zeros((1, ML_HEADS, 1, GATE_PAD), F32), None)
    yp, kp, vp, cp, np_, mp, convp = _layer(hp, xnp, 1, n_prompt, None, zero_state, wts, PROMPT_TM, N_META)

    hs = x_sample.reshape(nb * lq, D_MODEL)
    past = (cache_k[0].reshape(nb, -1, SB_HEAD_DIM), cache_v[0].reshape(nb, -1, SB_HEAD_DIM))
    s_state = (state_C[0], state_n[0][:, :, None, :],
               jnp.broadcast_to(state_m[0][:, :, None, None], (nb, ML_HEADS, 1, GATE_PAD)), state_conv[0])
    ys, ks, vs, cs, ns, ms, convs = _layer(hs, None, nb, lq, past, s_state, wts, 1024, 0)

    y_prompt = yp[None]
    y_sample = ys.reshape(nb, lq, D_MODEL)
    k_prompt = kp.reshape(1, 1, n_prompt, SB_HEADS, SB_HEAD_DIM)
    v_prompt = vp.reshape(1, 1, n_prompt, SB_HEADS, SB_HEAD_DIM)
    k_sample = ks.reshape(1, nb, lq, SB_HEADS, SB_HEAD_DIM)
    v_sample = vs.reshape(1, nb, lq, SB_HEADS, SB_HEAD_DIM)
    return (y_prompt, y_sample, k_prompt, v_prompt, cp[None], np_[None], mp[None], convp[None],
            k_sample, v_sample, cs[None], ns[None], ms[None], convs[None])
```

```python
import functools

import jax
import jax.numpy as jnp
from jax import lax
from jax.experimental import pallas as pl
from jax.experimental.pallas import tpu as pltpu

F32 = jnp.float32
BF16 = jnp.bfloat16

D_MODEL = 4096
N_META = 16
SB_HEADS = 16
SB_HEAD_DIM = 128
ML_HEADS = 4
ML_QK_DIM = 256
ML_V_DIM = 512
D_FF = 11008
CONV_W = 3
GATE_CAP = 15.0
NORM_EPS = 1e-6
CHUNK = 256

SB_W = SB_HEADS * SB_HEAD_DIM
ML_QK_W = ML_HEADS * ML_QK_DIM
ML_W = ML_HEADS * ML_V_DIM
MAIN_COLS = 3 * SB_W + 2 * ML_QK_W + 2 * ML_W
GATE_PAD = 128

NEG_BIG = -1e30
VMEM_LIMIT_BYTES = 48 * 1024 * 1024
VMEM_LIMIT_BIG_BYTES = 56 * 1024 * 1024

SB_TQ = 512
SB_TK = 256
SB_PAIR = 2
SB_DEAD = -120.0
PROMPT_TM = 1536
CONV_HALO = 8
CONV_ROWS = 64
FFN_ROWS = 768


def _params(*sem, vmem=VMEM_LIMIT_BYTES):
    return pltpu.CompilerParams(dimension_semantics=sem, vmem_limit_bytes=vmem)


def _split_dot(x, m, dims, passes):
    x_is_lhs, dn = dims
    acc = None
    rem = x
    for _ in range(passes):
        piece = rem.astype(BF16)
        rem = rem - piece.astype(F32)
        ops = (piece, m) if x_is_lhs else (m, piece)
        d = lax.dot_general(ops[0], ops[1], dn, preferred_element_type=F32)
        acc = d if acc is None else acc + d
    return acc


_NN = (((1,), (0,)), ((), ()))
_NT = (((1,), (1,)), ((), ()))
_TN = (((0,), (0,)), ((), ()))


def _softplus(z):
    return jnp.maximum(z, 0.0) + jnp.log(1.0 + jnp.exp(-jnp.abs(z)))


def _rmsnorm_rows(x, g):
    ms = jnp.mean(x * x, axis=-1, keepdims=True)
    return x * lax.rsqrt(ms + NORM_EPS) * g


def _rmsnorm_kernel(x_ref, g_ref, o_ref):
    o_ref[...] = _rmsnorm_rows(x_ref[...], g_ref[...]).astype(o_ref.dtype)


def _rmsnorm_skip_kernel(x_ref, nxt_ref, g_ref, o_ref, *, skip):
    x = jnp.concatenate([x_ref[skip:, :], nxt_ref[...]], axis=0)
    o_ref[...] = _rmsnorm_rows(x, g_ref[...]).astype(o_ref.dtype)


def _rmsnorm(x, g, out_dtype, tm, skip=0, rows_out=None):
    rows, d = x.shape
    g2 = g.reshape(1, d).astype(F32)
    if skip == 0:
        assert rows_out in (None, rows)
        return pl.pallas_call(
            _rmsnorm_kernel,
            grid=(pl.cdiv(rows, tm),),
            in_specs=[pl.BlockSpec((tm, d), lambda i: (i, 0)), pl.BlockSpec((1, d), lambda i: (0, 0))],
            out_specs=pl.BlockSpec((tm, d), lambda i: (i, 0)),
            out_shape=jax.ShapeDtypeStruct((rows, d), out_dtype),
            compiler_params=_params("parallel"),
            name="rmsnorm",
        )(x, g2)
    assert tm % skip == 0 and rows_out % tm == 0 and rows_out + skip <= rows
    per = tm // skip
    return pl.pallas_call(
        functools.partial(_rmsnorm_skip_kernel, skip=skip),
        grid=(rows_out // tm,),
        in_specs=[pl.BlockSpec((tm, d), lambda i: (i, 0)), pl.BlockSpec((skip, d), lambda i: ((i + 1) * per, 0)),
                  pl.BlockSpec((1, d), lambda i: (0, 0))],
        out_specs=pl.BlockSpec((tm, d), lambda i: (i, 0)),
        out_shape=jax.ShapeDtypeStruct((rows_out, d), out_dtype),
        compiler_params=_params("parallel"),
        name="rmsnorm_skip",
    )(x, x, g2)


def _rmsnorm_prompt_kernel(x_ref, prev_ref, meta_ref, g_ref, h_ref, o_ref, *, nmeta, nvalid):
    i = pl.program_id(0)
    tm = h_ref.shape[0]
    head = jnp.where(i == 0, meta_ref[...], prev_ref[...])
    h = jnp.concatenate([head, x_ref[0:tm - nmeta, :]], axis=0)
    row = i * tm + lax.broadcasted_iota(jnp.int32, (tm, 1), 0)
    h = jnp.where(row < nvalid, h, 0.0)
    h_ref[...] = h
    o_ref[...] = _rmsnorm_rows(h, g_ref[...]).astype(o_ref.dtype)


def _rmsnorm_prompt(x, meta, g, rows_out, tm):
    nx, d = x.shape
    nmeta = meta.shape[0]
    assert tm % nmeta == 0 and nx % tm == 0 and rows_out % tm == 0 and nmeta % 8 == 0
    per = tm // nmeta
    last_x = nx // tm - 1
    last_prev = nx // nmeta - 1
    kern = functools.partial(_rmsnorm_prompt_kernel, nmeta=nmeta, nvalid=nmeta + nx)
    return pl.pallas_call(
        kern,
        grid=(rows_out // tm,),
        in_specs=[pl.BlockSpec((tm, d), lambda i: (jnp.minimum(i, last_x), 0)),
                  pl.BlockSpec((nmeta, d), lambda i: (jnp.clip(i * per - 1, 0, last_prev), 0)),
                  pl.BlockSpec((nmeta, d), lambda i: (0, 0)),
                  pl.BlockSpec((1, d), lambda i: (0, 0))],
        out_specs=[pl.BlockSpec((tm, d), lambda i: (i, 0)), pl.BlockSpec((tm, d), lambda i: (i, 0))],
        out_shape=[jax.ShapeDtypeStruct((rows_out, d), F32), jax.ShapeDtypeStruct((rows_out, d), BF16)],
        compiler_params=_params("parallel"),
        name="rmsnorm_prompt",
    )(x, x, meta.astype(F32), g.reshape(1, d).astype(F32))


def _cast_kernel(x_ref, o_ref):
    o_ref[...] = x_ref[...].astype(o_ref.dtype)


def _cast_cols(w, col0, ncols, tk, tn):
    k = w.shape[0]
    assert k % tk == 0 and ncols % tn == 0 and col0 % tn == 0
    off = col0 // tn
    return pl.pallas_call(
        _cast_kernel,
        grid=(k // tk, ncols // tn),
        in_specs=[pl.BlockSpec((tk, tn), lambda i, j: (i, off + j))],
        out_specs=pl.BlockSpec((tk, tn), lambda i, j: (i, j)),
        out_shape=jax.ShapeDtypeStruct((k, ncols), BF16),
        compiler_params=_params("parallel", "parallel"),
        name="cast_cols",
    )(w)


def _cast_t_kernel(x_ref, o_ref):
    o_ref[...] = x_ref[...].T.astype(o_ref.dtype)


def _cast_cols_t(wt, col0, ncols, tk, tn):
    k = wt.shape[1]
    assert k % tk == 0 and ncols % tn == 0 and col0 % tn == 0
    off = col0 // tn
    return pl.pallas_call(
        _cast_t_kernel,
        grid=(k // tk, ncols // tn),
        in_specs=[pl.BlockSpec((tn, tk), lambda i, j: (off + j, i))],
        out_specs=pl.BlockSpec((tk, tn), lambda i, j: (i, j)),
        out_shape=jax.ShapeDtypeStruct((k, ncols), BF16),
        compiler_params=_params("parallel", "parallel"),
        name="cast_cols_t",
    )(wt)


def _cast_tiles(w, tk, tn):
    k, n = w.shape
    assert k % tk == 0 and n % tn == 0
    return pl.pallas_call(
        _cast_kernel,
        grid=(k // tk, n // tn),
        in_specs=[pl.BlockSpec((tk, tn), lambda i, j: (i, j))],
        out_specs=pl.BlockSpec((None, tk, tn), lambda i, j: (j, i, 0)),
        out_shape=jax.ShapeDtypeStruct((n // tn, k, tn), BF16),
        compiler_params=_params("parallel", "parallel"),
        name="cast_tiles",
    )(w)


def _mm_kernel(x_ref, w_ref, o_ref):
    o_ref[...] = jnp.dot(x_ref[...], w_ref[...], preferred_element_type=F32).astype(o_ref.dtype)


def _matmul(x, w, tm, tn, out_dtype=F32, name="matmul"):
    rows, k = x.shape
    n = w.shape[1]
    assert n % tn == 0
    return pl.pallas_call(
        _mm_kernel,
        grid=(pl.cdiv(rows, tm), n // tn),
        in_specs=[pl.BlockSpec((tm, k), lambda i, j: (i, 0)), pl.BlockSpec((k, tn), lambda i, j: (0, j))],
        out_specs=pl.BlockSpec((tm, tn), lambda i, j: (i, j)),
        out_shape=jax.ShapeDtypeStruct((rows, n), out_dtype),
        compiler_params=_params("parallel", "arbitrary"),
        name=name,
    )(x, w)


def _mm_nt_kernel(x_ref, wt_ref, o_ref):
    o_ref[...] = lax.dot_general(x_ref[...], wt_ref[...], _NT, preferred_element_type=F32)


def _matmul_nt(x, wt, tm, name):
    rows, k = x.shape
    n = wt.shape[0]
    return pl.pallas_call(
        _mm_nt_kernel,
        grid=(pl.cdiv(rows, tm),),
        in_specs=[pl.BlockSpec((tm, k), lambda i: (i, 0)), pl.BlockSpec((n, k), lambda i: (0, 0))],
        out_specs=pl.BlockSpec((tm, n), lambda i: (i, 0)),
        out_shape=jax.ShapeDtypeStruct((rows, n), F32),
        compiler_params=_params("parallel"),
        name=name,
    )(x, wt)


def _mm_dual_kernel(x_ref, w_ref, o32_ref, o16_ref):
    acc = jnp.dot(x_ref[...], w_ref[...], preferred_element_type=F32)
    o32_ref[...] = acc
    o16_ref[...] = acc.astype(BF16)


def _matmul_dual(x, w, tm, tn, rows32, name):
    rows, k = x.shape
    n = w.shape[1]
    assert n % tn == 0 and pl.cdiv(rows32, tm) == pl.cdiv(rows, tm)
    return pl.pallas_call(
        _mm_dual_kernel,
        grid=(pl.cdiv(rows, tm), n // tn),
        in_specs=[pl.BlockSpec((tm, k), lambda i, j: (i, 0)), pl.BlockSpec((k, tn), lambda i, j: (0, j))],
        out_specs=[pl.BlockSpec((tm, tn), lambda i, j: (i, j)), pl.BlockSpec((tm, tn), lambda i, j: (i, j))],
        out_shape=[jax.ShapeDtypeStruct((rows32, n), F32), jax.ShapeDtypeStruct((rows, n), BF16)],
        compiler_params=_params("parallel", "arbitrary"),
        name=name,
    )(x, w)


def _mm_res_kernel(x_ref, w_ref, r_ref, o_ref):
    o_ref[...] = r_ref[...] + jnp.dot(x_ref[...], w_ref[...], preferred_element_type=F32)


def _matmul_residual(x, w_tiles, res, tm, name, vmem=VMEM_LIMIT_BYTES):
    rows, k = x.shape
    ntiles, _, tn = w_tiles.shape
    n = ntiles * tn
    return pl.pallas_call(
        _mm_res_kernel,
        grid=(pl.cdiv(rows, tm), ntiles),
        in_specs=[pl.BlockSpec((tm, k), lambda i, j: (i, 0)), pl.BlockSpec((None, k, tn), lambda i, j: (j, 0, 0)),
                  pl.BlockSpec((tm, tn), lambda i, j: (i, j))],
        out_specs=pl.BlockSpec((tm, tn), lambda i, j: (i, j)),
        out_shape=jax.ShapeDtypeStruct((rows, n), F32),
        compiler_params=_params("parallel", "arbitrary", vmem=vmem),
        name=name,
    )(x, w_tiles, res)


def _mm2_res_kernel(a_ref, b_ref, wa_ref, wb_ref, r_ref, o_ref):
    acc = jnp.dot(a_ref[...], wa_ref[...], preferred_element_type=F32)
    acc = acc + jnp.dot(b_ref[...], wb_ref[...], preferred_element_type=F32)
    o_ref[...] = r_ref[...] + acc


def _out_proj(sb, ml, w_out, res, tm, tn):
    rows = sb.shape[0]
    return pl.pallas_call(
        _mm2_res_kernel,
        grid=(pl.cdiv(rows, tm), D_MODEL // tn),
        in_specs=[pl.BlockSpec((tm, SB_W), lambda i, j: (i, 0)), pl.BlockSpec((tm, ML_W), lambda i, j: (i, 0)),
                  pl.BlockSpec((SB_W, tn), lambda i, j: (0, j)), pl.BlockSpec((ML_W, tn), lambda i, j: (1, j)),
                  pl.BlockSpec((tm, tn), lambda i, j: (i, j))],
        out_specs=pl.BlockSpec((tm, tn), lambda i, j: (i, j)),
        out_shape=jax.ShapeDtypeStruct((rows, D_MODEL), F32),
        compiler_params=_params("parallel", "arbitrary"),
        name="out_proj",
    )(sb, ml, w_out, w_out, res)


def _sb_weights(s, tri, carry, mask):
    z = s * (SB_HEAD_DIM ** -0.5)
    nz = s * -(SB_HEAD_DIM ** -0.5)
    ls = jnp.minimum(nz, 0.0) - jnp.log(1.0 + jnp.exp(jnp.minimum(z, nz)))
    if mask is not None:
        ls = jnp.where(mask, ls, 0.0)
    after = _split_dot(ls, tri, (True, _NN), 2) + carry
    a = jnp.exp(z + ls + after)
    if mask is not None:
        a = jnp.where(mask, a, 0.0)
    return a, carry + jnp.sum(ls, axis=1, keepdims=True)


def _sb_block(q, k, v, tri, carry, mask):
    a, carry = _sb_weights(lax.dot_general(q, k, _NT, preferred_element_type=F32), tri, carry, mask)
    return jnp.dot(a.astype(BF16), v, preferred_element_type=F32), carry


def _strict_tri(n):
    j = lax.broadcasted_iota(jnp.int32, (n, n), 0)
    s = lax.broadcasted_iota(jnp.int32, (n, n), 1)
    return jnp.where(j > s, 1.0, 0.0).astype(BF16)


def _head_norm(o, g):
    ms = jnp.mean(o * o, axis=-1, keepdims=True)
    return o * lax.rsqrt(ms + NORM_EPS) * g


def _sb_live(carry):
    return (jnp.max(carry) > SB_DEAD).astype(jnp.int32)


def _sb_prompt_kernel(q_ref, k_ref, v_ref, g_ref, o_ref, acc_ref):
    i = pl.program_id(1)
    tri = _strict_tri(SB_TK)
    nmask = SB_TQ // SB_TK
    jlow = i * nmask
    rowi = lax.broadcasted_iota(jnp.int32, (SB_TQ, SB_TK), 0)
    coli = lax.broadcasted_iota(jnp.int32, (SB_TQ, SB_TK), 1)
    heads = [slice(p * SB_HEAD_DIM, (p + 1) * SB_HEAD_DIM) for p in range(SB_PAIR)]
    qs = [q_ref[:, c] for c in heads]

    def sweep(j, carries, mask, first):
        rows = pl.ds(pl.multiple_of(j * SB_TK, SB_TK), SB_TK)
        out = []
        for p, c in enumerate(heads):
            o_j, cy = _sb_block(qs[p], k_ref[rows, c], v_ref[rows, c], tri, carries[p], mask)
            if first:
                acc_ref[:, c] = o_j
            else:
                acc_ref[:, c] += o_j
            out.append(cy)
        return tuple(out)

    carries = tuple(jnp.zeros((SB_TQ, 1), F32) for _ in heads)
    for m in reversed(range(nmask)):
        carries = sweep(jlow + m, carries, m * SB_TK + coli < rowi, m == nmask - 1)

    def live(cs):
        return _sb_live(functools.reduce(jnp.maximum, cs))

    def cond(st):
        return jnp.logical_and(st[0] < jlow, st[2] > 0)

    def body(st):
        cs = sweep(jlow - 1 - st[0], st[1], None, False)
        return st[0] + 1, cs, live(cs)

    lax.while_loop(cond, body, (jnp.int32(0), carries, live(carries)))
    for p, c in enumerate(heads):
        o_ref[:, c] = _head_norm(acc_ref[:, c], g_ref[p:p + 1, :]).astype(o_ref.dtype)


def _sb_prompt(q, k, v, g):
    rows = q.shape[0]
    assert rows % SB_TQ == 0 and SB_TQ % SB_TK == 0 and k.shape[0] == rows and v.shape[0] == rows
    width = SB_PAIR * SB_HEAD_DIM
    return pl.pallas_call(
        _sb_prompt_kernel,
        grid=(SB_HEADS // SB_PAIR, rows // SB_TQ),
        in_specs=[pl.BlockSpec((SB_TQ, width), lambda h, i: (i, h)),
                  pl.BlockSpec((rows, width), lambda h, i: (0, h)),
                  pl.BlockSpec((rows, width), lambda h, i: (0, h)),
                  pl.BlockSpec((None, SB_PAIR, SB_HEAD_DIM), lambda h, i: (h, 0, 0))],
        out_specs=pl.BlockSpec((SB_TQ, width), lambda h, i: (i, h)),
        out_shape=jax.ShapeDtypeStruct((rows, SB_W), BF16),
        scratch_shapes=[pltpu.VMEM((SB_TQ, width), F32)],
        compiler_params=_params("parallel", "arbitrary"),
        name="sb_prompt",
    )(q, k, v, g.reshape(SB_HEADS // SB_PAIR, SB_PAIR, SB_HEAD_DIM).astype(F32))


def _sb_sample_kernel(q_ref, kn_ref, vn_ref, kc_hbm, vc_hbm, g_ref, o_ref, kbuf, vbuf, sem, acc_ref, *, lq, ncb):
    b = pl.program_id(0)
    blk_rows = SB_TK * SB_HEADS

    def head_cols(h):
        return slice(h * SB_HEAD_DIM, (h + 1) * SB_HEAD_DIM)

    qs = [q_ref[:, head_cols(h)] for h in range(SB_HEADS)]

    def sweep(keys, vals, tri, carry, mask, first):
        s = jnp.concatenate([lax.dot_general(qs[h], keys(h), _NT, preferred_element_type=F32)
                             for h in range(SB_HEADS)], axis=0)
        a, carry = _sb_weights(s, tri, carry, mask)
        a = a.astype(BF16)
        for h in range(SB_HEADS):
            o = jnp.dot(a[h * lq:(h + 1) * lq], vals(h), preferred_element_type=F32)
            if first:
                acc_ref[:, head_cols(h)] = o
            else:
                acc_ref[:, head_cols(h)] += o
        return carry

    qi = lax.rem(lax.broadcasted_iota(jnp.int32, (SB_HEADS * lq, lq), 0), lq)
    ki = lax.broadcasted_iota(jnp.int32, (SB_HEADS * lq, lq), 1)
    carry = sweep(lambda h: kn_ref[:, head_cols(h)], lambda h: vn_ref[:, head_cols(h)], _strict_tri(lq),
                  jnp.zeros((SB_HEADS * lq, 1), F32), ki < qi, True)
    tri = _strict_tri(SB_TK)

    def cond(st):
        return jnp.logical_and(st[0] < ncb, st[2] > 0)

    def body(st):
        row0 = pl.multiple_of((ncb - 1 - st[0]) * blk_rows, blk_rows)
        copy_k = pltpu.make_async_copy(kc_hbm.at[b, pl.ds(row0, blk_rows), :], kbuf, sem.at[0])
        copy_v = pltpu.make_async_copy(vc_hbm.at[b, pl.ds(row0, blk_rows), :], vbuf, sem.at[1])
        copy_k.start()
        copy_v.start()
        copy_k.wait()
        copy_v.wait()
        c = sweep(lambda h: kbuf[pl.ds(h, SB_TK, stride=SB_HEADS), :].astype(BF16),
                  lambda h: vbuf[pl.ds(h, SB_TK, stride=SB_HEADS), :].astype(BF16), tri, st[1], None, False)
        return st[0] + 1, c, _sb_live(c)

    lax.while_loop(cond, body, (jnp.int32(0), carry, _sb_live(carry)))
    for h in range(SB_HEADS):
        cols = head_cols(h)
        o_ref[:, cols] = _head_norm(acc_ref[:, cols], g_ref[h:h + 1, :]).astype(o_ref.dtype)


def _sb_sample(q, k, v, cache_k, cache_v, g, lq):
    nb, cache_rows, _ = cache_k.shape
    assert cache_rows % (SB_TK * SB_HEADS) == 0
    ncb = cache_rows // (SB_TK * SB_HEADS)
    kern = functools.partial(_sb_sample_kernel, lq=lq, ncb=ncb)
    return pl.pallas_call(
        kern,
        grid=(nb,),
        in_specs=[pl.BlockSpec((lq, SB_W), lambda b: (b, 0)),
                  pl.BlockSpec((lq, SB_W), lambda b: (b, 0)),
                  pl.BlockSpec((lq, SB_W), lambda b: (b, 0)),
                  pl.BlockSpec(memory_space=pl.ANY),
                  pl.BlockSpec(memory_space=pl.ANY),
                  pl.BlockSpec((SB_HEADS, SB_HEAD_DIM), lambda b: (0, 0))],
        out_specs=pl.BlockSpec((lq, SB_W), lambda b: (b, 0)),
        out_shape=jax.ShapeDtypeStruct((nb * lq, SB_W), BF16),
        scratch_shapes=[pltpu.VMEM((SB_TK * SB_HEADS, SB_HEAD_DIM), F32),
                        pltpu.VMEM((SB_TK * SB_HEADS, SB_HEAD_DIM), F32),
                        pltpu.SemaphoreType.DMA((2,)),
                        pltpu.VMEM((lq, SB_W), F32)],
        compiler_params=_params("arbitrary"),
        name="sb_sample",
    )(q, k, v, cache_k, cache_v, g.astype(F32))


def _mlstm_kernel(q_ref, k_ref, v_ref, og_ref, gt_ref, gb_ref, c0_ref, n0_ref, m0_ref, mlg_ref,
                  out_ref, c_ref, n_ref, m_ref, *, blk, nvalid):
    c = pl.program_id(1)

    @pl.when(c == 0)
    def _load_state():
        c_ref[...] = c0_ref[...]
        n_ref[...] = n0_ref[...]
        m_ref[...] = m0_ref[...]

    rows = c * blk + lax.broadcasted_iota(jnp.int32, (blk, 1), 0)
    valid = rows < nvalid
    cap = GATE_CAP * jnp.tanh((gt_ref[...] + gb_ref[...]) / GATE_CAP)
    logi = jnp.where(valid, cap, NEG_BIG)
    logf = jnp.where(valid, -_softplus(-cap), 0.0)

    ti = lax.broadcasted_iota(jnp.int32, (blk, blk), 0)
    si = lax.broadcasted_iota(jnp.int32, (blk, blk), 1)
    causal = si <= ti
    tril = jnp.where(causal, 1.0, 0.0).astype(BF16)
    b_all = _split_dot(logf, tril, (False, _NN), 3)

    lane = lax.broadcasted_iota(jnp.int32, (blk, GATE_PAD), 1)
    cols = jnp.where(lane < ML_HEADS, logi, b_all)
    er = lax.broadcasted_iota(jnp.int32, (16, GATE_PAD), 0)
    ec = lax.broadcasted_iota(jnp.int32, (16, GATE_PAD), 1)
    eye = jnp.where(er == ec, 1.0, 0.0).astype(BF16)
    rows_t = _split_dot(cols, eye, (False, _NT), 3)

    for h in range(ML_HEADS):
        li_col = logi[:, h:h + 1]
        b_col = b_all[:, ML_HEADS + h:ML_HEADS + h + 1]
        li_row = rows_t[h:h + 1, :]
        b_row = rows_t[ML_HEADS + h:ML_HEADS + h + 1, :]
        m_prev = m_ref[h][:, 0:1]
        d = jnp.where(causal, b_col - b_row + li_row, NEG_BIG)
        g_col = b_col + m_prev
        m_t = jnp.maximum(g_col, jnp.max(d, axis=1, keepdims=True))
        w_intra = jnp.exp(d - m_t)
        w_inter = jnp.exp(g_col - m_t)

        q = q_ref[:, h * ML_QK_DIM:(h + 1) * ML_QK_DIM] * (ML_QK_DIM ** -0.5)
        k = k_ref[:, h * ML_QK_DIM:(h + 1) * ML_QK_DIM]
        v = v_ref[:, h * ML_V_DIM:(h + 1) * ML_V_DIM]
        qb, kb, vb = q.astype(BF16), k.astype(BF16), v.astype(BF16)
        c_prev = c_ref[h]
        n_prev = n_ref[h]

        s_mat = lax.dot_general(qb, kb, _NT, preferred_element_type=F32) * w_intra
        num = jnp.dot(s_mat.astype(BF16), vb, preferred_element_type=F32)
        num = num + w_inter * jnp.dot(qb, c_prev.astype(BF16), preferred_element_type=F32)
        den = jnp.sum(s_mat, axis=1, keepdims=True) + w_inter * jnp.sum(q * n_prev, axis=1, keepdims=True)
        hv = num / jnp.maximum(jnp.abs(den), jnp.exp(-m_t))

        m_new = m_t[blk - 1:blk, :]
        w_end = jnp.exp(b_col[blk - 1:blk, :] - b_col + li_col - m_new)
        decay = jnp.exp(g_col[blk - 1:blk, :] - m_new)
        kw = k * w_end
        c_ref[h] = decay * c_prev + lax.dot_general(kw.astype(BF16), vb, _TN, preferred_element_type=F32)
        n_ref[h] = decay * n_prev + jnp.sum(kw, axis=0, keepdims=True)
        m_ref[h] = jnp.broadcast_to(m_new, (1, GATE_PAD))

        hn = _head_norm(hv, mlg_ref[h:h + 1, :])
        og = og_ref[:, h * ML_V_DIM:(h + 1) * ML_V_DIM]
        out_ref[:, h * ML_V_DIM:(h + 1) * ML_V_DIM] = (hn / (1.0 + jnp.exp(-og))).astype(out_ref.dtype)


def _mlstm(proj, gates, gbias, c0, n0, m0, ml_g, nstreams, blk, nvalid):
    rows = proj.shape[0]
    per_stream = rows // nstreams
    assert per_stream % blk == 0
    nchunks = per_stream // blk
    qk_blk = 0
    v_blk = (2 * ML_QK_W) // ML_W

    def row_map(col):
        return lambda s, c: (s * nchunks + c, col)

    def state_map(s, c):
        return (s, 0, 0, 0)

    kern = functools.partial(_mlstm_kernel, blk=blk, nvalid=nvalid)
    return pl.pallas_call(
        kern,
        grid=(nstreams, nchunks),
        in_specs=[pl.BlockSpec((blk, ML_QK_W), row_map(qk_blk)),
                  pl.BlockSpec((blk, ML_QK_W), row_map(qk_blk + 1)),
                  pl.BlockSpec((blk, ML_W), row_map(v_blk)),
                  pl.BlockSpec((blk, ML_W), row_map(v_blk + 1)),
                  pl.BlockSpec((blk, GATE_PAD), row_map(0)),
                  pl.BlockSpec((1, GATE_PAD), lambda s, c: (0, 0)),
                  pl.BlockSpec((None, ML_HEADS, ML_QK_DIM, ML_V_DIM), state_map),
                  pl.BlockSpec((None, ML_HEADS, 1, ML_QK_DIM), state_map),
                  pl.BlockSpec((None, ML_HEADS, 1, GATE_PAD), state_map),
                  pl.BlockSpec((ML_HEADS, ML_V_DIM), lambda s, c: (0, 0))],
        out_specs=[pl.BlockSpec((blk, ML_W), row_map(0)),
                   pl.BlockSpec((None, ML_HEADS, ML_QK_DIM, ML_V_DIM), state_map),
                   pl.BlockSpec((None, ML_HEADS, 1, ML_QK_DIM), state_map),
                   pl.BlockSpec((None, ML_HEADS, 1, GATE_PAD), state_map)],
        out_shape=[jax.ShapeDtypeStruct((rows, ML_W), BF16),
                   jax.ShapeDtypeStruct((nstreams, ML_HEADS, ML_QK_DIM, ML_V_DIM), F32),
                   jax.ShapeDtypeStruct((nstreams, ML_HEADS, 1, ML_QK_DIM), F32),
                   jax.ShapeDtypeStruct((nstreams, ML_HEADS, 1, GATE_PAD), F32)],
        compiler_params=_params("parallel", "arbitrary"),
        name="mlstm",
    )(proj, proj, proj, proj, gates, gbias, c0, n0, m0, ml_g.astype(F32))


def _silu_gate(cg, cv):
    return cg / (1.0 + jnp.exp(-cg)) * cv


def _conv_rows(hist, u, taps):
    ext = jnp.concatenate([hist, u], axis=0)
    acc = u * taps[CONV_W - 1]
    for d in range(1, CONV_W):
        acc = acc + pltpu.roll(ext, d, 0)[CONV_HALO:] * taps[CONV_W - 1 - d]
    return acc


def _ffn_up_prompt_kernel(x_ref, wg_ref, wv_ref, cwg_ref, cwv_ref, o_ref, sg_ref, sv_ref, tg_ref, tv_ref,
                          *, tm, state_row):
    i = pl.program_id(0)
    j = pl.program_id(1)

    @pl.when(jnp.logical_and(i == 0, j == 0))
    def _zero_history():
        tg_ref[...] = jnp.zeros(tg_ref.shape, F32)
        tv_ref[...] = jnp.zeros(tv_ref.shape, F32)

    taps_g = [cwg_ref[d:d + 1, :] for d in range(CONV_W)]
    taps_v = [cwv_ref[d:d + 1, :] for d in range(CONV_W)]
    hist_g = tg_ref[j]
    hist_v = tv_ref[j]
    for r0 in range(0, tm, FFN_ROWS):
        x = x_ref[r0:r0 + FFN_ROWS, :]
        ug = jnp.dot(x, wg_ref[...], preferred_element_type=F32)
        uv = jnp.dot(x, wv_ref[...], preferred_element_type=F32)
        act = _silu_gate(_conv_rows(hist_g, ug, taps_g), _conv_rows(hist_v, uv, taps_v))
        o_ref[r0:r0 + FFN_ROWS, :] = act.astype(o_ref.dtype)
        hist_g = ug[FFN_ROWS - CONV_HALO:]
        hist_v = uv[FFN_ROWS - CONV_HALO:]
        if r0 <= state_row < r0 + FFN_ROWS:
            sg_ref[...] = ug[state_row - r0:state_row - r0 + CONV_HALO]
            sv_ref[...] = uv[state_row - r0:state_row - r0 + CONV_HALO]
    tg_ref[j] = hist_g
    tv_ref[j] = hist_v


def _ffn_up_prompt(xn, w_gate, w_val, cw_gate, cw_val, tm, tn, nvalid):
    rows = xn.shape[0]
    assert rows % tm == 0 and tm % FFN_ROWS == 0 and nvalid % CONV_HALO == 0
    assert (nvalid - 1) // tm == rows // tm - 1
    nj = pl.cdiv(D_FF, tn)
    kern = functools.partial(_ffn_up_prompt_kernel, tm=tm, state_row=(nvalid - CONV_HALO) % tm)
    return pl.pallas_call(
        kern,
        grid=(rows // tm, nj),
        in_specs=[pl.BlockSpec((tm, D_MODEL), lambda i, j: (i, 0)),
                  pl.BlockSpec((D_MODEL, tn), lambda i, j: (0, j)),
                  pl.BlockSpec((D_MODEL, tn), lambda i, j: (0, j)),
                  pl.BlockSpec((CONV_W, tn), lambda i, j: (0, j)),
                  pl.BlockSpec((CONV_W, tn), lambda i, j: (0, j))],
        out_specs=[pl.BlockSpec((tm, tn), lambda i, j: (i, j)),
                   pl.BlockSpec((None, CONV_HALO, tn), lambda i, j: (i, 0, j)),
                   pl.BlockSpec((None, CONV_HALO, tn), lambda i, j: (i, 0, j))],
        out_shape=[jax.ShapeDtypeStruct((rows, D_FF), BF16),
                   jax.ShapeDtypeStruct((rows // tm, CONV_HALO, D_FF), F32),
                   jax.ShapeDtypeStruct((rows // tm, CONV_HALO, D_FF), F32)],
        scratch_shapes=[pltpu.VMEM((nj, CONV_HALO, tn), F32), pltpu.VMEM((nj, CONV_HALO, tn), F32)],
        compiler_params=_params("arbitrary", "arbitrary", vmem=VMEM_LIMIT_BIG_BYTES),
        name="ffn_up_prompt",
    )(xn, w_gate, w_val, cw_gate, cw_val)


def _ffn_up_sample_kernel(x_ref, sg_ref, sv_ref, wg_ref, wv_ref, cwg_ref, cwv_ref, o_ref, ng_ref, nv_ref,
                          eg_ref, ev_ref, *, nb, lq):
    x = x_ref[...]
    lo = CONV_HALO - (CONV_W - 1)
    for w_ref, s_ref, e_ref, n_ref in ((wg_ref, sg_ref, eg_ref, ng_ref), (wv_ref, sv_ref, ev_ref, nv_ref)):
        u = jnp.dot(x, w_ref[...], preferred_element_type=F32)
        e_ref[:, lo:CONV_HALO, :] = s_ref[...]
        e_ref[:, CONV_HALO:CONV_HALO + lq, :] = u.reshape(nb, lq, u.shape[-1])
        n_ref[...] = e_ref[:, lo + lq:CONV_HALO + lq, :]

    def conv(e_ref, cw_ref, r0, step):
        acc = None
        for d in range(CONV_W):
            term = e_ref[:, lo + d + r0:lo + d + r0 + step, :] * cw_ref[d:d + 1, :]
            acc = term if acc is None else acc + term
        return acc

    step = min(CONV_ROWS, lq)
    for r0 in range(0, lq, step):
        act = _silu_gate(conv(eg_ref, cwg_ref, r0, step), conv(ev_ref, cwv_ref, r0, step))
        for b in range(nb):
            o_ref[b * lq + r0:b * lq + r0 + step, :] = act[b].astype(o_ref.dtype)


def _ffn_up_sample(xn, state, w_gate, w_val, cw_gate, cw_val, nb, lq, tn):
    rows = xn.shape[0]
    assert rows == nb * lq and D_FF % tn == 0 and lq % min(CONV_ROWS, lq) == 0
    nj = D_FF // tn
    kern = functools.partial(_ffn_up_sample_kernel, nb=nb, lq=lq)
    return pl.pallas_call(
        kern,
        grid=(nj,),
        in_specs=[pl.BlockSpec((rows, D_MODEL), lambda j: (0, 0)),
                  pl.BlockSpec((nb, CONV_W - 1, tn), lambda j: (0, 0, j)),
                  pl.BlockSpec((nb, CONV_W - 1, tn), lambda j: (0, 0, nj + j)),
                  pl.BlockSpec((D_MODEL, tn), lambda j: (0, j)),
                  pl.BlockSpec((D_MODEL, tn), lambda j: (0, j)),
                  pl.BlockSpec((CONV_W, tn), lambda j: (0, j)),
                  pl.BlockSpec((CONV_W, tn), lambda j: (0, j))],
        out_specs=[pl.BlockSpec((rows, tn), lambda j: (0, j)),
                   pl.BlockSpec((nb, CONV_W - 1, tn), lambda j: (0, 0, j)),
                   pl.BlockSpec((nb, CONV_W - 1, tn), lambda j: (0, 0, j))],
        out_shape=[jax.ShapeDtypeStruct((rows, D_FF), BF16),
                   jax.ShapeDtypeStruct((nb, CONV_W - 1, D_FF), F32),
                   jax.ShapeDtypeStruct((nb, CONV_W - 1, D_FF), F32)],
        scratch_shapes=[pltpu.VMEM((nb, CONV_HALO + lq, tn), F32), pltpu.VMEM((nb, CONV_HALO + lq, tn), F32)],
        compiler_params=_params("arbitrary"),
        name="ffn_up_sample",
    )(xn, state, state, w_gate, w_val, cw_gate, cw_val)


def _layer(h, xn, nstreams, nvalid, past_kv, state, wts, tm, skip):
    (n1g, w_q, w_k, w_v, w_ml, w_gate, gbias, sb_g, ml_g, w_out, n2g, w_upg, w_upv, cw_g, cw_v, w_down,
     final_g) = wts
    rows = h.shape[0]
    per_stream = rows // nstreams
    rows_valid = (nstreams - 1) * per_stream + nvalid
    c0, n0, m0, conv0 = state

    if xn is None:
        xn = _rmsnorm(h, n1g, BF16, 256)
    q_b = _matmul(xn, w_q, tm, 512, out_dtype=BF16, name="q_proj")
    k_new, k_b = _matmul_dual(xn, w_k, tm, 512, rows_valid, name="k_proj")
    v_new, v_b = _matmul_dual(xn, w_v, tm, 512, rows_valid, name="v_proj")
    pm = _matmul(xn, w_ml, tm, 512, name="ml_proj")
    gates = _matmul_nt(xn, w_gate, tm, name="gate_proj")

    if past_kv is None:
        sb = _sb_prompt(q_b, k_b, v_b, sb_g)
        blk = CHUNK
    else:
        sb = _sb_sample(q_b, k_b, v_b, past_kv[0], past_kv[1], sb_g, per_stream)
        blk = per_stream
    ml, c_new, n_new, m_new = _mlstm(pm, gates, gbias, c0, n0, m0, ml_g, nstreams, blk, nvalid)

    h1 = _out_proj(sb, ml, w_out, h, tm, 512)
    xn2 = _rmsnorm(h1, n2g, BF16, 256)
    if conv0 is None:
        act, tail_g, tail_v = _ffn_up_prompt(xn2, w_upg, w_upv, cw_g, cw_v, tm, 512, nvalid)
        conv_new = jnp.concatenate([tail_g[-1], tail_v[-1]], axis=1)[None, CONV_HALO - (CONV_W - 1):]
    else:
        act, new_g, new_v = _ffn_up_sample(xn2, conv0, w_upg, w_upv, cw_g, cw_v, nstreams, per_stream, 256)
        conv_new = jnp.concatenate([new_g, new_v], axis=2)
    h2 = _matmul_residual(act, w_down, h1, 768 if rows % 768 == 0 else 512, name="down_proj",
                          vmem=VMEM_LIMIT_BIG_BYTES)
    y = _rmsnorm(h2, final_g, F32, 256, skip=skip, rows_out=nstreams * nvalid - skip)
    return y, k_new, v_new, c_new, n_new[:, :, 0, :], m_new[:, :, 0, 0], conv_new


def kernel(x_prompt, x_sample, cache_k, cache_v, state_C, state_n, state_m, state_conv, meta_tokens, norm1_g,
           w_in, b_igate, b_fgate, sb_head_g, ml_head_g, w_out, norm2_g, w_up, conv_w, w_down, final_g):
    bp, sp, _ = x_prompt.shape
    nb, lq, _ = x_sample.shape
    assert bp == 1 and w_in.shape[0] == 1
    n_prompt = N_META + sp
    rows_p = -(-n_prompt // PROMPT_TM) * PROMPT_TM
    assert rows_p % SB_TQ == 0 and rows_p % CHUNK == 0

    w_in_t = jnp.swapaxes(w_in, 1, 2)[0]
    wts = (norm1_g[0],
           _cast_cols_t(w_in_t, 0, SB_W, 512, 1024),
           _cast_cols_t(w_in_t, SB_W, SB_W, 512, 1024),
           _cast_cols_t(w_in_t, 2 * SB_W, SB_W, 512, 1024),
           _cast_cols_t(w_in_t, 3 * SB_W, MAIN_COLS - 3 * SB_W, 512, 1024),
           jnp.pad(w_in_t[MAIN_COLS:], ((0, GATE_PAD - 2 * ML_HEADS), (0, 0))).astype(BF16),
           jnp.pad(jnp.concatenate([b_igate[0], b_fgate[0]]), (0, GATE_PAD - 2 * ML_HEADS)).reshape(1, GATE_PAD)
           .astype(F32),
           sb_head_g[0], ml_head_g[0], w_out[0].astype(BF16), norm2_g[0],
           _cast_cols(w_up[0], 0, D_FF, 128, D_FF), _cast_cols(w_up[0], D_FF, D_FF, 128, D_FF),
           conv_w[0][:, :D_FF].astype(F32), conv_w[0][:, D_FF:].astype(F32),
           _cast_tiles(w_down[0], D_FF // 4, 256), final_g)

    hp, xnp = _rmsnorm_prompt(x_prompt[0], meta_tokens, norm1_g[0], rows_p, 256)
    zero_state = (jnp.zeros((1, ML_HEADS, ML_QK_DIM, ML_V_DIM), F32),
                  jnp.zeros((1, ML_HEADS, 1, ML_QK_DIM), F32),
                  jnp.zeros((1, ML_HEADS, 1, GATE_PAD), F32), None)
    yp, kp, vp, cp, np_, mp, convp = _layer(hp, xnp, 1, n_prompt, None, zero_state, wts, PROMPT_TM, N_META)

    hs = x_sample.reshape(nb * lq, D_MODEL)
    past = (cache_k[0].reshape(nb, -1, SB_HEAD_DIM), cache_v[0].reshape(nb, -1, SB_HEAD_DIM))
    s_state = (state_C[0], state_n[0][:, :, None, :],
               jnp.broadcast_to(state_m[0][:, :, None, None], (nb, ML_HEADS, 1, GATE_PAD)), state_conv[0])
    ys, ks, vs, cs, ns, ms, convs = _layer(hs, None, nb, lq, past, s_state, wts, 1024, 0)

    y_prompt = yp[None]
    y_sample = ys.reshape(nb, lq, D_MODEL)
    k_prompt = kp.reshape(1, 1, n_prompt, SB_HEADS, SB_HEAD_DIM)
    v_prompt = vp.reshape(1, 1, n_prompt, SB_HEADS, SB_HEAD_DIM)
    k_sample = ks.reshape(1, nb, lq, SB_HEADS, SB_HEAD_DIM)
    v_sample = vs.reshape(1, nb, lq, SB_HEADS, SB_HEAD_DIM)
    return (y_prompt, y_sample, k_prompt, v_prompt, cp[None], np_[None], mp[None], convp[None],
            k_sample, v_sample, cs[None], ns[None], ms[None], convs[None])
```

```python
import functools

import jax
import jax.numpy as jnp
from jax import lax
from jax.experimental import pallas as pl
from jax.experimental.pallas import tpu as pltpu

F32 = jnp.float32
BF16 = jnp.bfloat16

D_MODEL = 4096
N_META = 16
SB_HEADS = 16
SB_HEAD_DIM = 128
ML_HEADS = 4
ML_QK_DIM = 256
ML_V_DIM = 512
D_FF = 11008
CONV_W = 3
GATE_CAP = 15.0
NORM_EPS = 1e-6
CHUNK = 256

SB_W = SB_HEADS * SB_HEAD_DIM
ML_QK_W = ML_HEADS * ML_QK_DIM
ML_W = ML_HEADS * ML_V_DIM
MAIN_COLS = 3 * SB_W + 2 * ML_QK_W + 2 * ML_W
GATE_PAD = 128

NEG_BIG = -1e30
V7X_VMEM_BYTES = 64 * 1024 * 1024
VMEM_LIMIT_BYTES = V7X_VMEM_BYTES * 3 // 4
VMEM_LIMIT_BIG_BYTES = V7X_VMEM_BYTES * 7 // 8

SB_TQ = 512
SB_TK = 256
SB_PAIR = 2
SB_DEAD = -120.0
PROMPT_TM = 1536
CONV_HALO = 8
CONV_ROWS = 64
FFN_ROWS = 768


def _params(*sem, vmem=VMEM_LIMIT_BYTES):
    return pltpu.CompilerParams(dimension_semantics=sem, vmem_limit_bytes=vmem)


def _split_dot(x, m, dims, passes):
    x_is_lhs, dn = dims
    acc = None
    rem = x
    for _ in range(passes):
        piece = rem.astype(BF16)
        rem = rem - piece.astype(F32)
        ops = (piece, m) if x_is_lhs else (m, piece)
        d = lax.dot_general(ops[0], ops[1], dn, preferred_element_type=F32)
        acc = d if acc is None else acc + d
    return acc


_NN = (((1,), (0,)), ((), ()))
_NT = (((1,), (1,)), ((), ()))
_TN = (((0,), (0,)), ((), ()))


def _softplus(z):
    return jnp.maximum(z, 0.0) + jnp.log(1.0 + jnp.exp(-jnp.abs(z)))


def _rmsnorm_rows(x, g):
    ms = jnp.mean(x * x, axis=-1, keepdims=True)
    return x * lax.rsqrt(ms + NORM_EPS) * g


def _rmsnorm_kernel(x_ref, g_ref, o_ref):
    o_ref[...] = _rmsnorm_rows(x_ref[...], g_ref[...]).astype(o_ref.dtype)


def _rmsnorm_skip_kernel(x_ref, nxt_ref, g_ref, o_ref, *, skip):
    x = jnp.concatenate([x_ref[skip:, :], nxt_ref[...]], axis=0)
    o_ref[...] = _rmsnorm_rows(x, g_ref[...]).astype(o_ref.dtype)


def _rmsnorm(x, g, out_dtype, tm, skip=0, rows_out=None):
    rows, d = x.shape
    g2 = g.reshape(1, d).astype(F32)
    if skip == 0:
        assert rows_out in (None, rows)
        return pl.pallas_call(
            _rmsnorm_kernel,
            grid=(pl.cdiv(rows, tm),),
            in_specs=[pl.BlockSpec((tm, d), lambda i: (i, 0)), pl.BlockSpec((1, d), lambda i: (0, 0))],
            out_specs=pl.BlockSpec((tm, d), lambda i: (i, 0)),
            out_shape=jax.ShapeDtypeStruct((rows, d), out_dtype),
            compiler_params=_params("parallel"),
            name="rmsnorm",
        )(x, g2)
    assert tm % skip == 0 and rows_out % tm == 0 and rows_out + skip <= rows
    per = tm // skip
    return pl.pallas_call(
        functools.partial(_rmsnorm_skip_kernel, skip=skip),
        grid=(rows_out // tm,),
        in_specs=[pl.BlockSpec((tm, d), lambda i: (i, 0)), pl.BlockSpec((skip, d), lambda i: ((i + 1) * per, 0)),
                  pl.BlockSpec((1, d), lambda i: (0, 0))],
        out_specs=pl.BlockSpec((tm, d), lambda i: (i, 0)),
        out_shape=jax.ShapeDtypeStruct((rows_out, d), out_dtype),
        compiler_params=_params("parallel"),
        name="rmsnorm_skip",
    )(x, x, g2)


def _prompt_rows(i, x_ref, prev_ref, meta_ref, nvalid):
    tm = x_ref.shape[0]
    nmeta = meta_ref.shape[0]
    head = jnp.where(i == 0, meta_ref[...], prev_ref[...])
    h = jnp.concatenate([head, x_ref[0:tm - nmeta, :]], axis=0)
    row = i * tm + lax.broadcasted_iota(jnp.int32, (tm, 1), 0)
    return jnp.where(row < nvalid, h, 0.0)


def _prompt_row_specs(nx, nmeta, tm, width, col):
    per = tm // nmeta
    last_x = pl.cdiv(nx, tm) - 1
    last_prev = nx // nmeta - 1
    return [pl.BlockSpec((tm, width), lambda i, *j: (jnp.minimum(i, last_x), col(*j))),
            pl.BlockSpec((nmeta, width), lambda i, *j: (jnp.clip(i * per - 1, 0, last_prev), col(*j))),
            pl.BlockSpec((nmeta, width), lambda i, *j: (0, col(*j)))]


def _rmsnorm_prompt_kernel(x_ref, prev_ref, meta_ref, g_ref, o_ref, *, nvalid):
    h = _prompt_rows(pl.program_id(0), x_ref, prev_ref, meta_ref, nvalid)
    o_ref[...] = _rmsnorm_rows(h, g_ref[...]).astype(o_ref.dtype)


def _rmsnorm_prompt(x, meta, g, rows_out, tm):
    nx, d = x.shape
    nmeta = meta.shape[0]
    assert tm % nmeta == 0 and rows_out % tm == 0 and nmeta % 8 == 0
    kern = functools.partial(_rmsnorm_prompt_kernel, nvalid=nmeta + nx)
    return pl.pallas_call(
        kern,
        grid=(rows_out // tm,),
        in_specs=_prompt_row_specs(nx, nmeta, tm, d, lambda: 0) + [pl.BlockSpec((1, d), lambda i: (0, 0))],
        out_specs=pl.BlockSpec((tm, d), lambda i: (i, 0)),
        out_shape=jax.ShapeDtypeStruct((rows_out, d), BF16),
        compiler_params=_params("parallel"),
        name="rmsnorm_prompt",
    )(x, x, meta.astype(F32), g.reshape(1, d).astype(F32))


def _cast_kernel(x_ref, o_ref):
    o_ref[...] = x_ref[...].astype(o_ref.dtype)


def _cast_cols(w, col0, ncols, tk, tn):
    k = w.shape[0]
    assert k % tk == 0 and ncols % tn == 0 and col0 % tn == 0
    off = col0 // tn
    return pl.pallas_call(
        _cast_kernel,
        grid=(k // tk, ncols // tn),
        in_specs=[pl.BlockSpec((tk, tn), lambda i, j: (i, off + j))],
        out_specs=pl.BlockSpec((tk, tn), lambda i, j: (i, j)),
        out_shape=jax.ShapeDtypeStruct((k, ncols), BF16),
        compiler_params=_params("parallel", "parallel"),
        name="cast_cols",
    )(w)


def _cast_t_kernel(x_ref, o_ref):
    o_ref[...] = x_ref[...].T.astype(o_ref.dtype)


def _cast_cols_t(wt, col0, ncols, tk, tn):
    k = wt.shape[1]
    assert k % tk == 0 and ncols % tn == 0 and col0 % tn == 0
    off = col0 // tn
    return pl.pallas_call(
        _cast_t_kernel,
        grid=(k // tk, ncols // tn),
        in_specs=[pl.BlockSpec((tn, tk), lambda i, j: (off + j, i))],
        out_specs=pl.BlockSpec((tk, tn), lambda i, j: (i, j)),
        out_shape=jax.ShapeDtypeStruct((k, ncols), BF16),
        compiler_params=_params("parallel", "parallel"),
        name="cast_cols_t",
    )(wt)


def _cast_tiles(w, tk, tn):
    k, n = w.shape
    assert k % tk == 0 and n % tn == 0
    return pl.pallas_call(
        _cast_kernel,
        grid=(k // tk, n // tn),
        in_specs=[pl.BlockSpec((tk, tn), lambda i, j: (i, j))],
        out_specs=pl.BlockSpec((None, tk, tn), lambda i, j: (j, i, 0)),
        out_shape=jax.ShapeDtypeStruct((n // tn, k, tn), BF16),
        compiler_params=_params("parallel", "parallel"),
        name="cast_tiles",
    )(w)


def _mm_kernel(x_ref, w_ref, o_ref):
    o_ref[...] = jnp.dot(x_ref[...], w_ref[...], preferred_element_type=F32).astype(o_ref.dtype)


def _matmul(x, w, tm, tn, out_dtype=F32, name="matmul"):
    rows, k = x.shape
    n = w.shape[1]
    assert n % tn == 0
    return pl.pallas_call(
        _mm_kernel,
        grid=(pl.cdiv(rows, tm), n // tn),
        in_specs=[pl.BlockSpec((tm, k), lambda i, j: (i, 0)), pl.BlockSpec((k, tn), lambda i, j: (0, j))],
        out_specs=pl.BlockSpec((tm, tn), lambda i, j: (i, j)),
        out_shape=jax.ShapeDtypeStruct((rows, n), out_dtype),
        compiler_params=_params("parallel", "arbitrary"),
        name=name,
    )(x, w)


def _mm_nt_kernel(x_ref, wt_ref, o_ref):
    o_ref[...] = lax.dot_general(x_ref[...], wt_ref[...], _NT, preferred_element_type=F32)


def _matmul_nt(x, wt, tm, name):
    rows, k = x.shape
    n = wt.shape[0]
    return pl.pallas_call(
        _mm_nt_kernel,
        grid=(pl.cdiv(rows, tm),),
        in_specs=[pl.BlockSpec((tm, k), lambda i: (i, 0)), pl.BlockSpec((n, k), lambda i: (0, 0))],
        out_specs=pl.BlockSpec((tm, n), lambda i: (i, 0)),
        out_shape=jax.ShapeDtypeStruct((rows, n), F32),
        compiler_params=_params("parallel"),
        name=name,
    )(x, wt)


def _mm_dual_kernel(x_ref, w_ref, o32_ref, o16_ref):
    acc = jnp.dot(x_ref[...], w_ref[...], preferred_element_type=F32)
    o32_ref[...] = acc
    o16_ref[...] = acc.astype(BF16)


def _matmul_dual(x, w, tm, tn, rows32, name):
    rows, k = x.shape
    n = w.shape[1]
    assert n % tn == 0 and pl.cdiv(rows32, tm) == pl.cdiv(rows, tm)
    return pl.pallas_call(
        _mm_dual_kernel,
        grid=(pl.cdiv(rows, tm), n // tn),
        in_specs=[pl.BlockSpec((tm, k), lambda i, j: (i, 0)), pl.BlockSpec((k, tn), lambda i, j: (0, j))],
        out_specs=[pl.BlockSpec((tm, tn), lambda i, j: (i, j)), pl.BlockSpec((tm, tn), lambda i, j: (i, j))],
        out_shape=[jax.ShapeDtypeStruct((rows32, n), F32), jax.ShapeDtypeStruct((rows, n), BF16)],
        compiler_params=_params("parallel", "arbitrary"),
        name=name,
    )(x, w)


def _mm_res_kernel(x_ref, w_ref, r_ref, o_ref):
    o_ref[...] = r_ref[...] + jnp.dot(x_ref[...], w_ref[...], preferred_element_type=F32)


def _matmul_residual(x, w_tiles, res, tm, name, vmem=VMEM_LIMIT_BYTES):
    rows, k = x.shape
    ntiles, _, tn = w_tiles.shape
    n = ntiles * tn
    return pl.pallas_call(
        _mm_res_kernel,
        grid=(pl.cdiv(rows, tm), ntiles),
        in_specs=[pl.BlockSpec((tm, k), lambda i, j: (i, 0)), pl.BlockSpec((None, k, tn), lambda i, j: (j, 0, 0)),
                  pl.BlockSpec((tm, tn), lambda i, j: (i, j))],
        out_specs=pl.BlockSpec((tm, tn), lambda i, j: (i, j)),
        out_shape=jax.ShapeDtypeStruct((rows, n), F32),
        compiler_params=_params("parallel", "arbitrary", vmem=vmem),
        name=name,
    )(x, w_tiles, res)


def _mm2_res_kernel(a_ref, b_ref, wa_ref, wb_ref, r_ref, o_ref):
    acc = jnp.dot(a_ref[...], wa_ref[...], preferred_element_type=F32)
    acc = acc + jnp.dot(b_ref[...], wb_ref[...], preferred_element_type=F32)
    o_ref[...] = r_ref[...] + acc


def _mm2_prompt_res_kernel(a_ref, b_ref, wa_ref, wb_ref, x_ref, prev_ref, meta_ref, o_ref, *, nvalid):
    acc = jnp.dot(a_ref[...], wa_ref[...], preferred_element_type=F32)
    acc = acc + jnp.dot(b_ref[...], wb_ref[...], preferred_element_type=F32)
    o_ref[...] = _prompt_rows(pl.program_id(0), x_ref, prev_ref, meta_ref, nvalid) + acc


def _out_proj_prompt(sb, ml, w_out, x, meta, tm, tn):
    rows = sb.shape[0]
    nx = x.shape[0]
    nmeta = meta.shape[0]
    assert rows % tm == 0 and tm % nmeta == 0
    kern = functools.partial(_mm2_prompt_res_kernel, nvalid=nmeta + nx)
    return pl.pallas_call(
        kern,
        grid=(rows // tm, D_MODEL // tn),
        in_specs=[pl.BlockSpec((tm, SB_W), lambda i, j: (i, 0)), pl.BlockSpec((tm, ML_W), lambda i, j: (i, 0)),
                  pl.BlockSpec((SB_W, tn), lambda i, j: (0, j)), pl.BlockSpec((ML_W, tn), lambda i, j: (1, j))]
        + _prompt_row_specs(nx, nmeta, tm, tn, lambda j: j),
        out_specs=pl.BlockSpec((tm, tn), lambda i, j: (i, j)),
        out_shape=jax.ShapeDtypeStruct((rows, D_MODEL), F32),
        compiler_params=_params("parallel", "arbitrary"),
        name="out_proj",
    )(sb, ml, w_out, w_out, x, x, meta.astype(F32))


def _out_proj(sb, ml, w_out, res, tm, tn):
    rows = sb.shape[0]
    return pl.pallas_call(
        _mm2_res_kernel,
        grid=(pl.cdiv(rows, tm), D_MODEL // tn),
        in_specs=[pl.BlockSpec((tm, SB_W), lambda i, j: (i, 0)), pl.BlockSpec((tm, ML_W), lambda i, j: (i, 0)),
                  pl.BlockSpec((SB_W, tn), lambda i, j: (0, j)), pl.BlockSpec((ML_W, tn), lambda i, j: (1, j)),
                  pl.BlockSpec((tm, tn), lambda i, j: (i, j))],
        out_specs=pl.BlockSpec((tm, tn), lambda i, j: (i, j)),
        out_shape=jax.ShapeDtypeStruct((rows, D_MODEL), F32),
        compiler_params=_params("parallel", "arbitrary"),
        name="out_proj",
    )(sb, ml, w_out, w_out, res)


def _sb_weights(s, tri, carry, mask):
    z = s * (SB_HEAD_DIM ** -0.5)
    nz = s * -(SB_HEAD_DIM ** -0.5)
    ls = jnp.minimum(nz, 0.0) - jnp.log(1.0 + jnp.exp(jnp.minimum(z, nz)))
    if mask is not None:
        ls = jnp.where(mask, ls, 0.0)
    after = _split_dot(ls, tri, (True, _NN), 2) + carry
    a = jnp.exp(z + ls + after)
    if mask is not None:
        a = jnp.where(mask, a, 0.0)
    return a, carry + jnp.sum(ls, axis=1, keepdims=True)


def _sb_block(q, k, v, tri, carry, mask):
    a, carry = _sb_weights(lax.dot_general(q, k, _NT, preferred_element_type=F32), tri, carry, mask)
    return jnp.dot(a.astype(BF16), v, preferred_element_type=F32), carry


def _strict_tri(n):
    j = lax.broadcasted_iota(jnp.int32, (n, n), 0)
    s = lax.broadcasted_iota(jnp.int32, (n, n), 1)
    return jnp.where(j > s, 1.0, 0.0).astype(BF16)


def _head_norm(o, g):
    ms = jnp.mean(o * o, axis=-1, keepdims=True)
    return o * lax.rsqrt(ms + NORM_EPS) * g


def _sb_live(carry):
    return (jnp.max(carry) > SB_DEAD).astype(jnp.int32)


def _sb_prompt_kernel(q_ref, k_ref, v_ref, g_ref, o_ref, acc_ref):
    i = pl.program_id(1)
    tri = _strict_tri(SB_TK)
    nmask = SB_TQ // SB_TK
    jlow = i * nmask
    rowi = lax.broadcasted_iota(jnp.int32, (SB_TQ, SB_TK), 0)
    coli = lax.broadcasted_iota(jnp.int32, (SB_TQ, SB_TK), 1)
    heads = [slice(p * SB_HEAD_DIM, (p + 1) * SB_HEAD_DIM) for p in range(SB_PAIR)]
    qs = [q_ref[:, c] for c in heads]

    def sweep(j, carries, mask, first):
        rows = pl.ds(pl.multiple_of(j * SB_TK, SB_TK), SB_TK)
        out = []
        for p, c in enumerate(heads):
            o_j, cy = _sb_block(qs[p], k_ref[rows, c], v_ref[rows, c], tri, carries[p], mask)
            if first:
                acc_ref[:, c] = o_j
            else:
                acc_ref[:, c] += o_j
            out.append(cy)
        return tuple(out)

    carries = tuple(jnp.zeros((SB_TQ, 1), F32) for _ in heads)
    for m in reversed(range(nmask)):
        carries = sweep(jlow + m, carries, m * SB_TK + coli < rowi, m == nmask - 1)

    def live(cs):
        return _sb_live(functools.reduce(jnp.maximum, cs))

    def cond(st):
        return jnp.logical_and(st[0] < jlow, st[2] > 0)

    def body(st):
        cs = sweep(jlow - 1 - st[0], st[1], None, False)
        return st[0] + 1, cs, live(cs)

    lax.while_loop(cond, body, (jnp.int32(0), carries, live(carries)))
    for p, c in enumerate(heads):
        o_ref[:, c] = _head_norm(acc_ref[:, c], g_ref[p:p + 1, :]).astype(o_ref.dtype)


def _sb_prompt(q, k, v, g):
    rows = q.shape[0]
    assert rows % SB_TQ == 0 and SB_TQ % SB_TK == 0 and k.shape[0] == rows and v.shape[0] == rows
    width = SB_PAIR * SB_HEAD_DIM
    return pl.pallas_call(
        _sb_prompt_kernel,
        grid=(SB_HEADS // SB_PAIR, rows // SB_TQ),
        in_specs=[pl.BlockSpec((SB_TQ, width), lambda h, i: (i, h)),
                  pl.BlockSpec((rows, width), lambda h, i: (0, h)),
                  pl.BlockSpec((rows, width), lambda h, i: (0, h)),
                  pl.BlockSpec((None, SB_PAIR, SB_HEAD_DIM), lambda h, i: (h, 0, 0))],
        out_specs=pl.BlockSpec((SB_TQ, width), lambda h, i: (i, h)),
        out_shape=jax.ShapeDtypeStruct((rows, SB_W), BF16),
        scratch_shapes=[pltpu.VMEM((SB_TQ, width), F32)],
        compiler_params=_params("parallel", "arbitrary"),
        name="sb_prompt",
    )(q, k, v, g.reshape(SB_HEADS // SB_PAIR, SB_PAIR, SB_HEAD_DIM).astype(F32))


def _sb_sample_kernel(q_ref, kn_ref, vn_ref, kc_hbm, vc_hbm, g_ref, o_ref, kbuf, vbuf, sem, acc_ref, *, lq, ncb):
    b = pl.program_id(0)
    blk_rows = SB_TK * SB_HEADS

    def head_cols(h):
        return slice(h * SB_HEAD_DIM, (h + 1) * SB_HEAD_DIM)

    qs = [q_ref[:, head_cols(h)] for h in range(SB_HEADS)]

    def sweep(keys, vals, tri, carry, mask, first):
        s = jnp.concatenate([lax.dot_general(qs[h], keys(h), _NT, preferred_element_type=F32)
                             for h in range(SB_HEADS)], axis=0)
        a, carry = _sb_weights(s, tri, carry, mask)
        a = a.astype(BF16)
        for h in range(SB_HEADS):
            o = jnp.dot(a[h * lq:(h + 1) * lq], vals(h), preferred_element_type=F32)
            if first:
                acc_ref[:, head_cols(h)] = o
            else:
                acc_ref[:, head_cols(h)] += o
        return carry

    qi = lax.rem(lax.broadcasted_iota(jnp.int32, (SB_HEADS * lq, lq), 0), lq)
    ki = lax.broadcasted_iota(jnp.int32, (SB_HEADS * lq, lq), 1)
    carry = sweep(lambda h: kn_ref[:, head_cols(h)], lambda h: vn_ref[:, head_cols(h)], _strict_tri(lq),
                  jnp.zeros((SB_HEADS * lq, 1), F32), ki < qi, True)
    tri = _strict_tri(SB_TK)

    def cond(st):
        return jnp.logical_and(st[0] < ncb, st[2] > 0)

    def body(st):
        row0 = pl.multiple_of((ncb - 1 - st[0]) * blk_rows, blk_rows)
        copy_k = pltpu.make_async_copy(kc_hbm.at[b, pl.ds(row0, blk_rows), :], kbuf, sem.at[0])
        copy_v = pltpu.make_async_copy(vc_hbm.at[b, pl.ds(row0, blk_rows), :], vbuf, sem.at[1])
        copy_k.start()
        copy_v.start()
        copy_k.wait()
        copy_v.wait()
        c = sweep(lambda h: kbuf[pl.ds(h, SB_TK, stride=SB_HEADS), :].astype(BF16),
                  lambda h: vbuf[pl.ds(h, SB_TK, stride=SB_HEADS), :].astype(BF16), tri, st[1], None, False)
        return st[0] + 1, c, _sb_live(c)

    lax.while_loop(cond, body, (jnp.int32(0), carry, _sb_live(carry)))
    for h in range(SB_HEADS):
        cols = head_cols(h)
        o_ref[:, cols] = _head_norm(acc_ref[:, cols], g_ref[h:h + 1, :]).astype(o_ref.dtype)


def _sb_sample(q, k, v, cache_k, cache_v, g, lq):
    nb, cache_rows, _ = cache_k.shape
    assert cache_rows % (SB_TK * SB_HEADS) == 0
    ncb = cache_rows // (SB_TK * SB_HEADS)
    kern = functools.partial(_sb_sample_kernel, lq=lq, ncb=ncb)
    return pl.pallas_call(
        kern,
        grid=(nb,),
        in_specs=[pl.BlockSpec((lq, SB_W), lambda b: (b, 0)),
                  pl.BlockSpec((lq, SB_W), lambda b: (b, 0)),
                  pl.BlockSpec((lq, SB_W), lambda b: (b, 0)),
                  pl.BlockSpec(memory_space=pl.ANY),
                  pl.BlockSpec(memory_space=pl.ANY),
                  pl.BlockSpec((SB_HEADS, SB_HEAD_DIM), lambda b: (0, 0))],
        out_specs=pl.BlockSpec((lq, SB_W), lambda b: (b, 0)),
        out_shape=jax.ShapeDtypeStruct((nb * lq, SB_W), BF16),
        scratch_shapes=[pltpu.VMEM((SB_TK * SB_HEADS, SB_HEAD_DIM), F32),
                        pltpu.VMEM((SB_TK * SB_HEADS, SB_HEAD_DIM), F32),
                        pltpu.SemaphoreType.DMA((2,)),
                        pltpu.VMEM((lq, SB_W), F32)],
        compiler_params=_params("arbitrary"),
        name="sb_sample",
    )(q, k, v, cache_k, cache_v, g.astype(F32))


def _mlstm_kernel(q_ref, k_ref, v_ref, og_ref, gt_ref, gb_ref, c0_ref, n0_ref, m0_ref, mlg_ref,
                  out_ref, c_ref, n_ref, m_ref, *, blk, nvalid):
    c = pl.program_id(1)

    @pl.when(c == 0)
    def _load_state():
        c_ref[...] = c0_ref[...]
        n_ref[...] = n0_ref[...]
        m_ref[...] = m0_ref[...]

    rows = c * blk + lax.broadcasted_iota(jnp.int32, (blk, 1), 0)
    valid = rows < nvalid
    cap = GATE_CAP * jnp.tanh((gt_ref[...] + gb_ref[...]) / GATE_CAP)
    logi = jnp.where(valid, cap, NEG_BIG)
    logf = jnp.where(valid, -_softplus(-cap), 0.0)

    ti = lax.broadcasted_iota(jnp.int32, (blk, blk), 0)
    si = lax.broadcasted_iota(jnp.int32, (blk, blk), 1)
    causal = si <= ti
    tril = jnp.where(causal, 1.0, 0.0).astype(BF16)
    b_all = _split_dot(logf, tril, (False, _NN), 3)

    lane = lax.broadcasted_iota(jnp.int32, (blk, GATE_PAD), 1)
    cols = jnp.where(lane < ML_HEADS, logi, b_all)
    er = lax.broadcasted_iota(jnp.int32, (16, GATE_PAD), 0)
    ec = lax.broadcasted_iota(jnp.int32, (16, GATE_PAD), 1)
    eye = jnp.where(er == ec, 1.0, 0.0).astype(BF16)
    rows_t = _split_dot(cols, eye, (False, _NT), 3)

    for h in range(ML_HEADS):
        li_col = logi[:, h:h + 1]
        b_col = b_all[:, ML_HEADS + h:ML_HEADS + h + 1]
        li_row = rows_t[h:h + 1, :]
        b_row = rows_t[ML_HEADS + h:ML_HEADS + h + 1, :]
        m_prev = m_ref[h][:, 0:1]
        d = jnp.where(causal, b_col - b_row + li_row, NEG_BIG)
        g_col = b_col + m_prev
        m_t = jnp.maximum(g_col, jnp.max(d, axis=1, keepdims=True))
        w_intra = jnp.exp(d - m_t)
        w_inter = jnp.exp(g_col - m_t)

        q = q_ref[:, h * ML_QK_DIM:(h + 1) * ML_QK_DIM] * (ML_QK_DIM ** -0.5)
        k = k_ref[:, h * ML_QK_DIM:(h + 1) * ML_QK_DIM]
        v = v_ref[:, h * ML_V_DIM:(h + 1) * ML_V_DIM]
        qb, kb, vb = q.astype(BF16), k.astype(BF16), v.astype(BF16)
        c_prev = c_ref[h]
        n_prev = n_ref[h]

        s_mat = lax.dot_general(qb, kb, _NT, preferred_element_type=F32) * w_intra
        num = jnp.dot(s_mat.astype(BF16), vb, preferred_element_type=F32)
        num = num + w_inter * jnp.dot(qb, c_prev.astype(BF16), preferred_element_type=F32)
        den = jnp.sum(s_mat, axis=1, keepdims=True) + w_inter * jnp.sum(q * n_prev, axis=1, keepdims=True)
        hv = num / jnp.maximum(jnp.abs(den), jnp.exp(-m_t))

        m_new = m_t[blk - 1:blk, :]
        w_end = jnp.exp(b_col[blk - 1:blk, :] - b_col + li_col - m_new)
        decay = jnp.exp(g_col[blk - 1:blk, :] - m_new)
        kw = k * w_end
        c_ref[h] = decay * c_prev + lax.dot_general(kw.astype(BF16), vb, _TN, preferred_element_type=F32)
        n_ref[h] = decay * n_prev + jnp.sum(kw, axis=0, keepdims=True)
        m_ref[h] = jnp.broadcast_to(m_new, (1, GATE_PAD))

        hn = _head_norm(hv, mlg_ref[h:h + 1, :])
        og = og_ref[:, h * ML_V_DIM:(h + 1) * ML_V_DIM]
        out_ref[:, h * ML_V_DIM:(h + 1) * ML_V_DIM] = (hn / (1.0 + jnp.exp(-og))).astype(out_ref.dtype)


def _mlstm(proj, gates, gbias, c0, n0, m0, ml_g, nstreams, blk, nvalid):
    rows = proj.shape[0]
    per_stream = rows // nstreams
    assert per_stream % blk == 0
    nchunks = per_stream // blk
    qk_blk = 0
    v_blk = (2 * ML_QK_W) // ML_W

    def row_map(col):
        return lambda s, c: (s * nchunks + c, col)

    def state_map(s, c):
        return (s, 0, 0, 0)

    kern = functools.partial(_mlstm_kernel, blk=blk, nvalid=nvalid)
    return pl.pallas_call(
        kern,
        grid=(nstreams, nchunks),
        in_specs=[pl.BlockSpec((blk, ML_QK_W), row_map(qk_blk)),
                  pl.BlockSpec((blk, ML_QK_W), row_map(qk_blk + 1)),
                  pl.BlockSpec((blk, ML_W), row_map(v_blk)),
                  pl.BlockSpec((blk, ML_W), row_map(v_blk + 1)),
                  pl.BlockSpec((blk, GATE_PAD), row_map(0)),
                  pl.BlockSpec((1, GATE_PAD), lambda s, c: (0, 0)),
                  pl.BlockSpec((None, ML_HEADS, ML_QK_DIM, ML_V_DIM), state_map),
                  pl.BlockSpec((None, ML_HEADS, 1, ML_QK_DIM), state_map),
                  pl.BlockSpec((None, ML_HEADS, 1, GATE_PAD), state_map),
                  pl.BlockSpec((ML_HEADS, ML_V_DIM), lambda s, c: (0, 0))],
        out_specs=[pl.BlockSpec((blk, ML_W), row_map(0)),
                   pl.BlockSpec((None, ML_HEADS, ML_QK_DIM, ML_V_DIM), state_map),
                   pl.BlockSpec((None, ML_HEADS, 1, ML_QK_DIM), state_map),
                   pl.BlockSpec((None, ML_HEADS, 1, GATE_PAD), state_map)],
        out_shape=[jax.ShapeDtypeStruct((rows, ML_W), BF16),
                   jax.ShapeDtypeStruct((nstreams, ML_HEADS, ML_QK_DIM, ML_V_DIM), F32),
                   jax.ShapeDtypeStruct((nstreams, ML_HEADS, 1, ML_QK_DIM), F32),
                   jax.ShapeDtypeStruct((nstreams, ML_HEADS, 1, GATE_PAD), F32)],
        compiler_params=_params("parallel", "arbitrary"),
        name="mlstm",
    )(proj, proj, proj, proj, gates, gbias, c0, n0, m0, ml_g.astype(F32))


def _silu_gate(cg, cv):
    return cg / (1.0 + jnp.exp(-cg)) * cv


def _conv_rows(hist, u, taps):
    ext = jnp.concatenate([hist, u], axis=0)
    acc = u * taps[CONV_W - 1]
    for d in range(1, CONV_W):
        acc = acc + pltpu.roll(ext, d, 0)[CONV_HALO:] * taps[CONV_W - 1 - d]
    return acc


def _ffn_up_prompt_kernel(x_ref, wg_ref, wv_ref, cwg_ref, cwv_ref, o_ref, sg_ref, sv_ref, tg_ref, tv_ref,
                          *, tm, state_row):
    i = pl.program_id(0)
    j = pl.program_id(1)

    @pl.when(jnp.logical_and(i == 0, j == 0))
    def _zero_history():
        tg_ref[...] = jnp.zeros(tg_ref.shape, F32)
        tv_ref[...] = jnp.zeros(tv_ref.shape, F32)

    taps_g = [cwg_ref[d:d + 1, :] for d in range(CONV_W)]
    taps_v = [cwv_ref[d:d + 1, :] for d in range(CONV_W)]
    hist_g = tg_ref[j]
    hist_v = tv_ref[j]
    for r0 in range(0, tm, FFN_ROWS):
        x = x_ref[r0:r0 + FFN_ROWS, :]
        ug = jnp.dot(x, wg_ref[...], preferred_element_type=F32)
        uv = jnp.dot(x, wv_ref[...], preferred_element_type=F32)
        act = _silu_gate(_conv_rows(hist_g, ug, taps_g), _conv_rows(hist_v, uv, taps_v))
        o_ref[r0:r0 + FFN_ROWS, :] = act.astype(o_ref.dtype)
        hist_g = ug[FFN_ROWS - CONV_HALO:]
        hist_v = uv[FFN_ROWS - CONV_HALO:]
        if r0 <= state_row < r0 + FFN_ROWS:
            sg_ref[...] = ug[state_row - r0:state_row - r0 + CONV_HALO]
            sv_ref[...] = uv[state_row - r0:state_row - r0 + CONV_HALO]
    tg_ref[j] = hist_g
    tv_ref[j] = hist_v


def _ffn_up_prompt(xn, w_gate, w_val, cw_gate, cw_val, tm, tn, nvalid):
    rows = xn.shape[0]
    assert rows % tm == 0 and tm % FFN_ROWS == 0 and nvalid % CONV_HALO == 0
    assert (nvalid - 1) // tm == rows // tm - 1
    nj = pl.cdiv(D_FF, tn)
    kern = functools.partial(_ffn_up_prompt_kernel, tm=tm, state_row=(nvalid - CONV_HALO) % tm)
    return pl.pallas_call(
        kern,
        grid=(rows // tm, nj),
        in_specs=[pl.BlockSpec((tm, D_MODEL), lambda i, j: (i, 0)),
                  pl.BlockSpec((D_MODEL, tn), lambda i, j: (0, j)),
                  pl.BlockSpec((D_MODEL, tn), lambda i, j: (0, j)),
                  pl.BlockSpec((CONV_W, tn), lambda i, j: (0, j)),
                  pl.BlockSpec((CONV_W, tn), lambda i, j: (0, j))],
        out_specs=[pl.BlockSpec((tm, tn), lambda i, j: (i, j)),
                   pl.BlockSpec((None, CONV_HALO, tn), lambda i, j: (i, 0, j)),
                   pl.BlockSpec((None, CONV_HALO, tn), lambda i, j: (i, 0, j))],
        out_shape=[jax.ShapeDtypeStruct((rows, D_FF), BF16),
                   jax.ShapeDtypeStruct((rows // tm, CONV_HALO, D_FF), F32),
                   jax.ShapeDtypeStruct((rows // tm, CONV_HALO, D_FF), F32)],
        scratch_shapes=[pltpu.VMEM((nj, CONV_HALO, tn), F32), pltpu.VMEM((nj, CONV_HALO, tn), F32)],
        compiler_params=_params("arbitrary", "arbitrary", vmem=VMEM_LIMIT_BIG_BYTES),
        name="ffn_up_prompt",
    )(xn, w_gate, w_val, cw_gate, cw_val)


def _ffn_up_sample_kernel(x_ref, sg_ref, sv_ref, wg_ref, wv_ref, cwg_ref, cwv_ref, o_ref, ng_ref, nv_ref,
                          eg_ref, ev_ref, *, nb, lq):
    x = x_ref[...]
    lo = CONV_HALO - (CONV_W - 1)
    for w_ref, s_ref, e_ref, n_ref in ((wg_ref, sg_ref, eg_ref, ng_ref), (wv_ref, sv_ref, ev_ref, nv_ref)):
        u = jnp.dot(x, w_ref[...], preferred_element_type=F32)
        e_ref[:, lo:CONV_HALO, :] = s_ref[...]
        e_ref[:, CONV_HALO:CONV_HALO + lq, :] = u.reshape(nb, lq, u.shape[-1])
        n_ref[...] = e_ref[:, lo + lq:CONV_HALO + lq, :]

    def conv(e_ref, cw_ref, r0, step):
        acc = None
        for d in range(CONV_W):
            term = e_ref[:, lo + d + r0:lo + d + r0 + step, :] * cw_ref[d:d + 1, :]
            acc = term if acc is None else acc + term
        return acc

    step = min(CONV_ROWS, lq)
    for r0 in range(0, lq, step):
        act = _silu_gate(conv(eg_ref, cwg_ref, r0, step), conv(ev_ref, cwv_ref, r0, step))
        for b in range(nb):
            o_ref[b * lq + r0:b * lq + r0 + step, :] = act[b].astype(o_ref.dtype)


def _ffn_up_sample(xn, state, w_gate, w_val, cw_gate, cw_val, nb, lq, tn):
    rows = xn.shape[0]
    assert rows == nb * lq and D_FF % tn == 0 and lq % min(CONV_ROWS, lq) == 0
    nj = D_FF // tn
    kern = functools.partial(_ffn_up_sample_kernel, nb=nb, lq=lq)
    return pl.pallas_call(
        kern,
        grid=(nj,),
        in_specs=[pl.BlockSpec((rows, D_MODEL), lambda j: (0, 0)),
                  pl.BlockSpec((nb, CONV_W - 1, tn), lambda j: (0, 0, j)),
                  pl.BlockSpec((nb, CONV_W - 1, tn), lambda j: (0, 0, nj + j)),
                  pl.BlockSpec((D_MODEL, tn), lambda j: (0, j)),
                  pl.BlockSpec((D_MODEL, tn), lambda j: (0, j)),
                  pl.BlockSpec((CONV_W, tn), lambda j: (0, j)),
                  pl.BlockSpec((CONV_W, tn), lambda j: (0, j))],
        out_specs=[pl.BlockSpec((rows, tn), lambda j: (0, j)),
                   pl.BlockSpec((nb, CONV_W - 1, tn), lambda j: (0, 0, j)),
                   pl.BlockSpec((nb, CONV_W - 1, tn), lambda j: (0, 0, j))],
        out_shape=[jax.ShapeDtypeStruct((rows, D_FF), BF16),
                   jax.ShapeDtypeStruct((nb, CONV_W - 1, D_FF), F32),
                   jax.ShapeDtypeStruct((nb, CONV_W - 1, D_FF), F32)],
        scratch_shapes=[pltpu.VMEM((nb, CONV_HALO + lq, tn), F32), pltpu.VMEM((nb, CONV_HALO + lq, tn), F32)],
        compiler_params=_params("arbitrary"),
        name="ffn_up_sample",
    )(xn, state, state, w_gate, w_val, cw_gate, cw_val)


def _layer(h, xn, nstreams, nvalid, past_kv, state, wts, tm, skip):
    (n1g, w_q, w_k, w_v, w_ml, w_gate, gbias, sb_g, ml_g, w_out, n2g, w_upg, w_upv, cw_g, cw_v, w_down,
     final_g) = wts
    rows = h.shape[0] if xn is None else xn.shape[0]
    per_stream = rows // nstreams
    rows_valid = (nstreams - 1) * per_stream + nvalid
    c0, n0, m0, conv0 = state

    if xn is None:
        xn = _rmsnorm(h, n1g, BF16, 256)
    q_b = _matmul(xn, w_q, tm, 512, out_dtype=BF16, name="q_proj")
    k_new, k_b = _matmul_dual(xn, w_k, tm, 512, rows_valid, name="k_proj")
    v_new, v_b = _matmul_dual(xn, w_v, tm, 512, rows_valid, name="v_proj")
    pm = _matmul(xn, w_ml, tm, 512, name="ml_proj")
    gates = _matmul_nt(xn, w_gate, tm, name="gate_proj")

    if past_kv is None:
        sb = _sb_prompt(q_b, k_b, v_b, sb_g)
        blk = CHUNK
    else:
        sb = _sb_sample(q_b, k_b, v_b, past_kv[0], past_kv[1], sb_g, per_stream)
        blk = per_stream
    ml, c_new, n_new, m_new = _mlstm(pm, gates, gbias, c0, n0, m0, ml_g, nstreams, blk, nvalid)

    if isinstance(h, tuple):
        h1 = _out_proj_prompt(sb, ml, w_out, h[0], h[1], tm, 512)
    else:
        h1 = _out_proj(sb, ml, w_out, h, tm, 512)
    xn2 = _rmsnorm(h1, n2g, BF16, 256)
    if conv0 is None:
        act, tail_g, tail_v = _ffn_up_prompt(xn2, w_upg, w_upv, cw_g, cw_v, tm, 512, nvalid)
        conv_new = jnp.concatenate([tail_g[-1], tail_v[-1]], axis=1)[None, CONV_HALO - (CONV_W - 1):]
    else:
        act, new_g, new_v = _ffn_up_sample(xn2, conv0, w_upg, w_upv, cw_g, cw_v, nstreams, per_stream, 256)
        conv_new = jnp.concatenate([new_g, new_v], axis=2)
    h2 = _matmul_residual(act, w_down, h1, 768 if rows % 768 == 0 else 512, name="down_proj",
                          vmem=VMEM_LIMIT_BIG_BYTES)
    y = _rmsnorm(h2, final_g, F32, 256, skip=skip, rows_out=nstreams * nvalid - skip)
    return y, k_new, v_new, c_new, n_new[:, :, 0, :], m_new[:, :, 0, 0], conv_new


def kernel(x_prompt, x_sample, cache_k, cache_v, state_C, state_n, state_m, state_conv, meta_tokens, norm1_g,
           w_in, b_igate, b_fgate, sb_head_g, ml_head_g, w_out, norm2_g, w_up, conv_w, w_down, final_g):
    bp, sp, _ = x_prompt.shape
    nb, lq, _ = x_sample.shape
    assert bp == 1 and w_in.shape[0] == 1
    n_prompt = N_META + sp
    rows_p = -(-n_prompt // PROMPT_TM) * PROMPT_TM
    assert rows_p % SB_TQ == 0 and rows_p % CHUNK == 0

    w_in_t = jnp.swapaxes(w_in, 1, 2)[0]
    wts = (norm1_g[0],
           _cast_cols_t(w_in_t, 0, SB_W, 512, 1024),
           _cast_cols_t(w_in_t, SB_W, SB_W, 512, 1024),
           _cast_cols_t(w_in_t, 2 * SB_W, SB_W, 512, 1024),
           _cast_cols_t(w_in_t, 3 * SB_W, MAIN_COLS - 3 * SB_W, 512, 1024),
           jnp.pad(w_in_t[MAIN_COLS:], ((0, GATE_PAD - 2 * ML_HEADS), (0, 0))).astype(BF16),
           jnp.pad(jnp.concatenate([b_igate[0], b_fgate[0]]), (0, GATE_PAD - 2 * ML_HEADS)).reshape(1, GATE_PAD)
           .astype(F32),
           sb_head_g[0], ml_head_g[0], w_out[0].astype(BF16), norm2_g[0],
           _cast_cols(w_up[0], 0, D_FF, 128, D_FF), _cast_cols(w_up[0], D_FF, D_FF, 128, D_FF),
           conv_w[0][:, :D_FF].astype(F32), conv_w[0][:, D_FF:].astype(F32),
           _cast_tiles(w_down[0], D_FF // 4, 256), final_g)

    xnp = _rmsnorm_prompt(x_prompt[0], meta_tokens, norm1_g[0], rows_p, 256)
    zero_state = (jnp.zeros((1, ML_HEADS, ML_QK_DIM, ML_V_DIM), F32),
                  jnp.zeros((1, ML_HEADS, 1, ML_QK_DIM), F32),
                  jnp.zeros((1, ML_HEADS, 1, GATE_PAD), F32), None)
    yp, kp, vp, cp, np_, mp, convp = _layer((x_prompt[0], meta_tokens), xnp, 1, n_prompt, None, zero_state, wts,
                                            PROMPT_TM, N_META)

    hs = x_sample.reshape(nb * lq, D_MODEL)
    past = (cache_k[0].reshape(nb, -1, SB_HEAD_DIM), cache_v[0].reshape(nb, -1, SB_HEAD_DIM))
    s_state = (state_C[0], state_n[0][:, :, None, :],
               jnp.broadcast_to(state_m[0][:, :, None, None], (nb, ML_HEADS, 1, GATE_PAD)), state_conv[0])
    ys, ks, vs, cs, ns, ms, convs = _layer(hs, None, nb, lq, past, s_state, wts, 1024, 0)

    y_prompt = yp[None]
    y_sample = ys.reshape(nb, lq, D_MODEL)
    k_prompt = kp.reshape(1, 1, n_prompt, SB_HEADS, SB_HEAD_DIM)
    v_prompt = vp.reshape(1, 1, n_prompt, SB_HEADS, SB_HEAD_DIM)
    k_sample = ks.reshape(1, nb, lq, SB_HEADS, SB_HEAD_DIM)
    v_sample = vs.reshape(1, nb, lq, SB_HEADS, SB_HEAD_DIM)
    return (y_prompt, y_sample, k_prompt, v_prompt, cp[None], np_[None], mp[None], convp[None],
            k_sample, v_sample, cs[None], ns[None], ms[None], convs[None])
```
